```python
import math, functools
import jax, jax.numpy as jnp
from jax import lax
import numpy as np

D_MODEL = 4096
BATCH = 4
SEQ = 2048
DEPTH = 2
DEC_BATCH = 32
DEC_SEQ = 1
PAST_LEN = 16384
PAGE_SIZE = 128

N_META = 16
HEAD_DIM = 64
ATT_HEADS = 32
ATT_KV_HEADS = 4
ATT_GROUP = ATT_HEADS // ATT_KV_HEADS
ATT_WIDTH = ATT_HEADS * HEAD_DIM
KV_WIDTH = ATT_KV_HEADS * HEAD_DIM
WINDOW = 128
BLOCK = 128
SCALE = HEAD_DIM ** -0.5
RWKV_HEADS = 32
RWKV_WIDTH = RWKV_HEADS * HEAD_DIM
DECAY_LORA = 96
ICLR_LORA = 96
GATE_LORA = 256
SHIFT_COLS = 3 * RWKV_WIDTH + DECAY_LORA + ICLR_LORA + GATE_LORA
IN_COLS = ATT_WIDTH + 2 * KV_WIDTH + SHIFT_COLS + 2 * D_MODEL
IN_SPLITS = (ATT_WIDTH, ATT_WIDTH + KV_WIDTH, ATT_WIDTH + 2 * KV_WIDTH, ATT_WIDTH + 2 * KV_WIDTH + SHIFT_COLS)
RWKV_SPLITS = (RWKV_WIDTH, 2 * RWKV_WIDTH, 3 * RWKV_WIDTH, 3 * RWKV_WIDTH + DECAY_LORA,
               3 * RWKV_WIDTH + DECAY_LORA + ICLR_LORA)
D_FF_DENSE = 11008
N_EXPERTS = 8
TOP_K = 2
D_FF_EXPERT = 14336
N_DENSE = (DEPTH + 1) // 2
N_MOE = DEPTH // 2
ALPHA = (2 * DEPTH) ** 0.25
BETA = (8 * DEPTH) ** -0.25
LN_EPS = 1e-5
GN_EPS = 64e-5

kernel_name = 'hybrid_swa_sink_rwkv7_moe_decoder_step'


def layer_norm(x, g, b):
    xf = x.astype(jnp.float32)
    mu = jnp.mean(xf, axis=-1, keepdims=True)
    var = jnp.mean(jnp.square(xf - mu), axis=-1, keepdims=True)
    return ((xf - mu) * lax.rsqrt(var + LN_EPS) * g.astype(jnp.float32) + b.astype(jnp.float32)).astype(x.dtype)


def sink_softmax(s, sink):
    m = jnp.maximum(jnp.max(s, axis=-1, keepdims=True), sink)
    p = jnp.exp(s - m)
    return p / (jnp.sum(p, axis=-1, keepdims=True) + jnp.exp(sink - m))


def swa_prompt(q, k, v, sinks):
    b, seq_len = q.shape[:2]
    lead = (-N_META) % BLOCK
    tail = (-(lead + seq_len)) % BLOCK
    nb = (lead + seq_len + tail) // BLOCK
    qb = jnp.pad(q, ((0, 0), (lead, tail), (0, 0), (0, 0))).reshape(b, nb, BLOCK, ATT_KV_HEADS, ATT_GROUP, HEAD_DIM)
    kv_pad = ((0, 0), (lead + BLOCK, tail), (0, 0), (0, 0))
    kb = jnp.pad(k, kv_pad).reshape(b, nb + 1, BLOCK, ATT_KV_HEADS, HEAD_DIM)
    vb = jnp.pad(v, kv_pad).reshape(b, nb + 1, BLOCK, ATT_KV_HEADS, HEAD_DIM)
    kc = jnp.concatenate([kb[:, :-1], kb[:, 1:]], axis=2)
    vc = jnp.concatenate([vb[:, :-1], vb[:, 1:]], axis=2)
    s = jnp.einsum('bnqkgd,bnskd->bnkgqs', qb, kc, preferred_element_type=jnp.float32) * SCALE
    blk = jnp.arange(nb)[:, None, None]
    q_pos = blk * BLOCK + jnp.arange(BLOCK)[None, :, None] - lead
    k_pos = blk * BLOCK + jnp.arange(2 * BLOCK)[None, None, :] - BLOCK - lead
    diff = q_pos - k_pos
    mask = (k_pos >= 0) & (diff >= 0) & (diff < WINDOW)
    s = jnp.where(mask[None, :, None, None], s, -jnp.inf)
    sink = sinks.astype(jnp.float32).reshape(ATT_KV_HEADS, ATT_GROUP)[None, None, :, :, None, None]
    p = sink_softmax(s, sink)
    o = jnp.einsum('bnkgqs,bnskd->bnqkgd', p.astype(v.dtype), vc)
    o = o.reshape(b, nb * BLOCK, ATT_WIDTH)[:, lead:lead + seq_len]
    return o, k[:, -WINDOW:], v[:, -WINDOW:]


def swa_sample(q, k, v, k_buf, v_buf, sinks):
    b, t = q.shape[:2]
    kc = jnp.concatenate([k_buf.astype(k.dtype), k], axis=1)
    vc = jnp.concatenate([v_buf.astype(v.dtype), v], axis=1)
    qg = q.reshape(b, t, ATT_KV_HEADS, ATT_GROUP, HEAD_DIM)
    s = jnp.einsum('btkgd,bskd->bkgts', qg, kc, preferred_element_type=jnp.float32) * SCALE
    diff = jnp.arange(t)[:, None] + WINDOW - jnp.arange(WINDOW + t)[None, :]
    mask = (diff >= 0) & (diff < WINDOW)
    s = jnp.where(mask, s, -jnp.inf)
    sink = sinks.astype(jnp.float32).reshape(ATT_KV_HEADS, ATT_GROUP)[None, :, :, None, None]
    p = sink_softmax(s, sink)
    o = jnp.einsum('bkgts,bskd->btkgd', p.astype(v.dtype), vc).reshape(b, t, ATT_WIDTH)
    return o, kc[:, t:], vc[:, t:]


def attend_prompt(q, k, v, sinks):
    return swa_prompt(q, k, v, sinks)


def rwkv7_time_mix(u, shift_prev, s0, lp):
    b, t, _ = u.shape
    f32 = jnp.float32
    u_prev = jnp.concatenate([shift_prev[:, None].astype(u.dtype), u[:, :-1]], axis=1)
    us = u + (u_prev - u) * lp['mu'].astype(u.dtype)
    r, k, v, wd, ad, gd = jnp.split(us, RWKV_SPLITS, axis=-1)

    def heads(z):
        return z.astype(f32).reshape(b, t, RWKV_HEADS, HEAD_DIM)

    def hparam(z):
        return z.astype(f32).reshape(RWKV_HEADS, HEAD_DIM)

    w_log = -jax.nn.softplus(-(lp['w0'].astype(f32) + jnp.tanh(wd.astype(f32)) @ lp['w2'].astype(f32))) - 0.5
    decay = heads(jnp.exp(-jnp.exp(w_log)))
    a = heads(jax.nn.sigmoid(lp['a0'].astype(f32) + ad.astype(f32) @ lp['a2'].astype(f32)))
    g = (jax.nn.sigmoid(gd) @ lp['g2']).astype(f32)
    r, k, v = heads(r), heads(k), heads(v)
    kk = k * hparam(lp['k_k'])
    kk = kk / jnp.maximum(jnp.sqrt(jnp.sum(kk * kk, axis=-1, keepdims=True)), 1e-12)
    k = k * (1.0 + (a - 1.0) * hparam(lp['k_a']))

    def step(S, inp):
        r_t, w_t, k_t, v_t, kk_t, a_t = inp
        sa = jnp.einsum('bhvk,bhk->bhv', S, -kk_t)
        S = S * w_t[:, :, None, :] + sa[..., None] * (kk_t * a_t)[:, :, None, :] + v_t[..., None] * k_t[:, :, None, :]
        return S, jnp.einsum('bhvk,bhk->bhv', S, r_t)

    xs = tuple(jnp.moveaxis(z, 1, 0) for z in (r, decay, k, v, kk, a))
    s_fin, y = lax.scan(step, s0.astype(f32), xs)
    y = jnp.moveaxis(y, 0, 1)
    mu = jnp.mean(y, axis=-1, keepdims=True)
    var = jnp.mean(jnp.square(y - mu), axis=-1, keepdims=True)
    y = (y - mu) * lax.rsqrt(var + GN_EPS) * hparam(lp['gn_w']) + hparam(lp['gn_b'])
    y = y + jnp.sum(r * k * hparam(lp['r_k']), axis=-1, keepdims=True) * v
    y = y.reshape(b, t, RWKV_WIDTH) * g
    return y.astype(u.dtype), u[:, -1], s_fin.astype(u.dtype)


def swiglu(x, w_gate, w_up, w_down):
    return (jax.nn.silu(x @ w_gate) * (x @ w_up)) @ w_down


def moe_swiglu(x, router, w_gate, w_up, w_down):
    logits = jnp.einsum('btd,de->bte', x, router, preferred_element_type=jnp.float32)
    top_val, top_idx = lax.top_k(logits, TOP_K)
    gate = jax.nn.softmax(top_val, axis=-1)
    combine = jnp.sum(jax.nn.one_hot(top_idx, N_EXPERTS, dtype=jnp.float32) * gate[..., None], axis=-2)
    y = jnp.zeros(x.shape, jnp.float32)
    for e in range(N_EXPERTS):
        y = y + combine[..., e:e + 1] * swiglu(x, w_gate[e], w_up[e], w_down[e]).astype(jnp.float32)
    return y.astype(x.dtype)


def trunk_layer(x, lp, ffn, attend, shift_prev, s0):
    b, t, _ = x.shape
    f32 = jnp.float32
    proj = x @ lp['w_in']
    q, k, v, u, gates = jnp.split(proj, IN_SPLITS, axis=-1)
    y_att, k_buf, v_buf = attend(q.reshape(b, t, ATT_HEADS, HEAD_DIM),
                                 k.reshape(b, t, ATT_KV_HEADS, HEAD_DIM),
                                 v.reshape(b, t, ATT_KV_HEADS, HEAD_DIM))
    y_rwkv, shift_new, s_new = rwkv7_time_mix(u, shift_prev, s0, lp)
    gates = jax.nn.sigmoid(gates.astype(f32))
    merged = (gates[..., :D_MODEL] * (y_att @ lp['p_att']).astype(f32)
              + gates[..., D_MODEL:] * (y_rwkv @ lp['p_rwkv']).astype(f32))
    mix = merged.astype(x.dtype) @ lp['w_out']
    x = layer_norm(ALPHA * x + mix, lp['ln_mix_g'], lp['ln_mix_b'])
    x = layer_norm(ALPHA * x + ffn(x), lp['ln_ffn_g'], lp['ln_ffn_b'])
    return x, k_buf, v_buf, s_new, shift_new


def setup_inputs(seed: int = 0) -> dict:
    key = jax.random.key(seed)
    keys = iter(jax.random.split(key, 64))
    f32 = jnp.float32

    def nrm(shape, scale=1.0):
        return jax.random.normal(next(keys), shape, f32) * scale

    def gain(shape):
        return 1.0 + nrm(shape, 0.05)

    return {
        'x_prompt': nrm((BATCH, SEQ, D_MODEL)),
        'x_sample': nrm((DEC_BATCH, DEC_SEQ, D_MODEL)),
        'cache_swa_k': nrm((DEPTH, DEC_BATCH, WINDOW, ATT_KV_HEADS, HEAD_DIM)),
        'cache_swa_v': nrm((DEPTH, DEC_BATCH, WINDOW, ATT_KV_HEADS, HEAD_DIM)),
        'state_rwkv': nrm((DEPTH, DEC_BATCH, RWKV_HEADS, HEAD_DIM, HEAD_DIM), 0.5),
        'state_shift': nrm((DEPTH, DEC_BATCH, SHIFT_COLS)),
        'meta_tokens': nrm((N_META, D_MODEL)),
        'ln_emb_g': gain((D_MODEL,)),
        'ln_emb_b': nrm((D_MODEL,), 0.02),
        'w_in': nrm((DEPTH, D_MODEL, IN_COLS), D_MODEL ** -0.5),
        'att_sinks': nrm((DEPTH, ATT_HEADS), 0.5),
        'rwkv_mu': jax.random.uniform(next(keys), (DEPTH, SHIFT_COLS), f32),
        'rwkv_w0': nrm((DEPTH, RWKV_WIDTH), 0.5),
        'rwkv_w2': nrm((DEPTH, DECAY_LORA, RWKV_WIDTH), 0.5 * DECAY_LORA ** -0.5),
        'rwkv_a0': nrm((DEPTH, RWKV_WIDTH), 0.1),
        'rwkv_a2': nrm((DEPTH, ICLR_LORA, RWKV_WIDTH), ICLR_LORA ** -0.5),
        'rwkv_g2': nrm((DEPTH, GATE_LORA, RWKV_WIDTH), GATE_LORA ** -0.5),
        'rwkv_k_k': 0.85 + nrm((DEPTH, RWKV_WIDTH), 0.05),
        'rwkv_k_a': gain((DEPTH, RWKV_WIDTH)),
        'rwkv_r_k': nrm((DEPTH, RWKV_WIDTH), 0.1),
        'rwkv_gn_w': gain((DEPTH, RWKV_WIDTH)),
        'rwkv_gn_b': nrm((DEPTH, RWKV_WIDTH), 0.02),
        'w_proj_att': nrm((DEPTH, ATT_WIDTH, D_MODEL), ATT_WIDTH ** -0.5),
        'w_proj_rwkv': nrm((DEPTH, RWKV_WIDTH, D_MODEL), RWKV_WIDTH ** -0.5),
        'w_out': nrm((DEPTH, D_MODEL, D_MODEL), BETA * D_MODEL ** -0.5),
        'ln_mix_g': gain((DEPTH, D_MODEL)),
        'ln_mix_b': nrm((DEPTH, D_MODEL), 0.02),
        'ln_ffn_g': gain((DEPTH, D_MODEL)),
        'ln_ffn_b': nrm((DEPTH, D_MODEL), 0.02),
        'ffn_w_gate': nrm((N_DENSE, D_MODEL, D_FF_DENSE), D_MODEL ** -0.5),
        'ffn_w_up': nrm((N_DENSE, D_MODEL, D_FF_DENSE), D_MODEL ** -0.5),
        'ffn_w_down': nrm((N_DENSE, D_FF_DENSE, D_MODEL), BETA * D_FF_DENSE ** -0.5),
        'moe_router': nrm((N_MOE, D_MODEL, N_EXPERTS), D_MODEL ** -0.5),
        'moe_w_gate': nrm((N_MOE, N_EXPERTS, D_MODEL, D_FF_EXPERT), D_MODEL ** -0.5),
        'moe_w_up': nrm((N_MOE, N_EXPERTS, D_MODEL, D_FF_EXPERT), D_MODEL ** -0.5),
        'moe_w_down': nrm((N_MOE, N_EXPERTS, D_FF_EXPERT, D_MODEL), BETA * D_FF_EXPERT ** -0.5),
    }


def reference(x_prompt, x_sample, cache_swa_k, cache_swa_v, state_rwkv, state_shift,
              meta_tokens, ln_emb_g, ln_emb_b, w_in, att_sinks, rwkv_mu, rwkv_w0, rwkv_w2,
              rwkv_a0, rwkv_a2, rwkv_g2, rwkv_k_k, rwkv_k_a, rwkv_r_k, rwkv_gn_w, rwkv_gn_b,
              w_proj_att, w_proj_rwkv, w_out, ln_mix_g, ln_mix_b, ln_ffn_g, ln_ffn_b,
              ffn_w_gate, ffn_w_up, ffn_w_down, moe_router, moe_w_gate, moe_w_up, moe_w_down):
    b = x_prompt.shape[0]
    meta = jnp.broadcast_to(meta_tokens[None].astype(x_prompt.dtype), (b, N_META, D_MODEL))
    xp = layer_norm(jnp.concatenate([meta, x_prompt], axis=1), ln_emb_g, ln_emb_b)
    xs = layer_norm(x_sample, ln_emb_g, ln_emb_b)
    pk, pv, ps, psh, sk, sv, ss, ssh = [], [], [], [], [], [], [], []
    for l in range(DEPTH):
        lp = {'w_in': w_in[l], 'mu': rwkv_mu[l], 'w0': rwkv_w0[l], 'w2': rwkv_w2[l],
              'a0': rwkv_a0[l], 'a2': rwkv_a2[l], 'g2': rwkv_g2[l], 'k_k': rwkv_k_k[l],
              'k_a': rwkv_k_a[l], 'r_k': rwkv_r_k[l], 'gn_w': rwkv_gn_w[l], 'gn_b': rwkv_gn_b[l],
              'p_att': w_proj_att[l], 'p_rwkv': w_proj_rwkv[l], 'w_out': w_out[l],
              'ln_mix_g': ln_mix_g[l], 'ln_mix_b': ln_mix_b[l],
              'ln_ffn_g': ln_ffn_g[l], 'ln_ffn_b': ln_ffn_b[l]}
        if l % 2 == 0:
            i = l // 2
            ffn = functools.partial(swiglu, w_gate=ffn_w_gate[i], w_up=ffn_w_up[i], w_down=ffn_w_down[i])
        else:
            i = l // 2
            ffn = functools.partial(moe_swiglu, router=moe_router[i], w_gate=moe_w_gate[i],
                                    w_up=moe_w_up[i], w_down=moe_w_down[i])
        xp, k_n, v_n, s_n, sh_n = trunk_layer(
            xp, lp, ffn, functools.partial(attend_prompt, sinks=att_sinks[l]),
            jnp.zeros((b, SHIFT_COLS), xp.dtype),
            jnp.zeros((b, RWKV_HEADS, HEAD_DIM, HEAD_DIM), jnp.float32))
        pk.append(k_n); pv.append(v_n); ps.append(s_n); psh.append(sh_n)
        xs, k_n, v_n, s_n, sh_n = trunk_layer(
            xs, lp, ffn,
            functools.partial(swa_sample, k_buf=cache_swa_k[l], v_buf=cache_swa_v[l], sinks=att_sinks[l]),
            state_shift[l], state_rwkv[l])
        sk.append(k_n); sv.append(v_n); ss.append(s_n); ssh.append(sh_n)
    y_prompt = xp[:, N_META:]
    y_sample = xs
    new_k_prompt = jnp.stack(pk)
    new_v_prompt = jnp.stack(pv)
    new_rwkv_prompt = jnp.stack(ps)
    new_shift_prompt = jnp.stack(psh)
    new_k_sample = jnp.stack(sk)
    new_v_sample = jnp.stack(sv)
    new_rwkv_sample = jnp.stack(ss)
    new_shift_sample = jnp.stack(ssh)
    return (y_prompt, y_sample, new_k_prompt, new_v_prompt, new_rwkv_prompt, new_shift_prompt,
            new_k_sample, new_v_sample, new_rwkv_sample, new_shift_sample)
```

```python
import functools
import math

import jax
import jax.numpy as jnp
from jax import lax
from jax.experimental import pallas as pl
from jax.experimental.pallas import tpu as pltpu

F32 = jnp.float32
BF16 = jnp.bfloat16
I32 = jnp.int32

LANES = 128
SUBLANES = 8
VMEM_LIMIT = 56 * 1024 * 1024
LN_EPS = 1e-5
GN_EPS = 64e-5
NEG_BIG = -1e30
HIGHEST = lax.Precision.HIGHEST


def _cparams(*sem):
    return pltpu.CompilerParams(dimension_semantics=sem, vmem_limit_bytes=VMEM_LIMIT)


def _round_up(x, m):
    return (x + m - 1) // m * m


def _div_tile(n, target, mult):
    best = None
    for t in range(mult, min(n, target) + 1, mult):
        if n % t == 0:
            best = t
    assert best is not None, (n, target, mult)
    return best


def _ln_rows(x, g, b):
    mu = jnp.mean(x, axis=-1, keepdims=True)
    xc = x - mu
    var = jnp.mean(xc * xc, axis=-1, keepdims=True)
    return xc * lax.rsqrt(var + LN_EPS) * g + b


def _ln_kernel(x_ref, g_ref, b_ref, o_ref, ob_ref):
    y = _ln_rows(x_ref[...], g_ref[...], b_ref[...])
    o_ref[...] = y
    ob_ref[...] = y.astype(BF16)


def _res_ln_kernel(x_ref, y_ref, g_ref, b_ref, o_ref, ob_ref, *, alpha):
    y = _ln_rows(alpha * x_ref[...] + y_ref[...], g_ref[...], b_ref[...])
    o_ref[...] = y
    ob_ref[...] = y.astype(BF16)


def _layer_norm(x, g, b, y=None, alpha=None):
    r, d = x.shape
    tr = _div_tile(r, 256, SUBLANES)
    row = pl.BlockSpec((tr, d), lambda i: (i, 0))
    vec = pl.BlockSpec((1, d), lambda i: (0, 0))
    out_shape = (jax.ShapeDtypeStruct((r, d), F32), jax.ShapeDtypeStruct((r, d), BF16))
    if y is None:
        return pl.pallas_call(
            _ln_kernel, grid=(r // tr,), in_specs=[row, vec, vec], out_specs=(row, row),
            out_shape=out_shape, compiler_params=_cparams("parallel"), name="ln",
        )(x, g.reshape(1, d), b.reshape(1, d))
    return pl.pallas_call(
        functools.partial(_res_ln_kernel, alpha=alpha), grid=(r // tr,),
        in_specs=[row, row, vec, vec], out_specs=(row, row), out_shape=out_shape,
        compiler_params=_cparams("parallel"), name="res_ln",
    )(x, y, g.reshape(1, d), b.reshape(1, d))


SUB_ROWS = 256
KACC_COLS = 1024


def _tile_rows(valid, tm, fn, zero_fn):
    @pl.when(valid == tm)
    def _():
        fn(slice(None))

    @pl.when(valid < tm)
    def _():
        for s in range(tm // SUB_ROWS):
            rows = pl.ds(s * SUB_ROWS, SUB_ROWS)

            @pl.when(s * SUB_ROWS < valid)
            def _():
                fn(rows)

            @pl.when(s * SUB_ROWS >= valid)
            def _():
                zero_fn(rows)


def _ws_kernel(valid_ref, tidx_ref, texp_ref, wnew_ref, *refs, n_lhs, n_w, n_extra, pairs,
               epilogue, tm):
    lhs = refs[:n_lhs]
    ws = refs[n_lhs:n_lhs + n_w]
    extras = refs[n_lhs + n_w:n_lhs + n_w + n_extra]
    o_ref = refs[n_lhs + n_w + n_extra]
    wb = refs[n_lhs + n_w + n_extra + 1:]
    i = pl.program_id(1)

    @pl.when(jnp.logical_or(i == 0, wnew_ref[i] == 1))
    def _():
        for w_ref, wb_ref in zip(ws, wb):
            wb_ref[...] = w_ref[...].astype(BF16)

    def compute(rows):
        accs = [jnp.dot(lhs[li][rows, :], wb[wi][...], preferred_element_type=F32)
                for li, wi in pairs]
        ex = [e[rows, :] for e in extras]
        o_ref[rows, :] = epilogue(accs, ex).astype(o_ref.dtype)

    def zero(rows):
        o_ref[rows, :] = jnp.zeros((SUB_ROWS, o_ref.shape[1]), o_ref.dtype)

    _tile_rows(valid_ref[i], tm, compute, zero)


def _matmul_ws(lhs, ws, pairs, extras, epilogue, out_dtype, n_out, *, tm, tn, sched, name):
    valid, tidx, texp, wnew = sched
    rows, _ = lhs[0].shape
    nm = valid.shape[0]
    tn = min(tn, n_out)
    nn = pl.cdiv(n_out, tn)

    def lhs_map(j, i, v, t, e, wn):
        return (t[i], 0)

    def w_map(lead):
        def m(j, i, v, t, e, wn):
            return tuple(e[i] if s == "e" else s for s in lead) + (0, j)
        return m

    def out_map(j, i, v, t, e, wn):
        return (i, j)

    in_specs = [pl.BlockSpec((tm, x.shape[1]), lhs_map) for x in lhs]
    w_scratch = []
    for arr, lead in ws:
        k = arr.shape[-2]
        in_specs.append(pl.BlockSpec((None,) * len(lead) + (k, tn), w_map(lead)))
        w_scratch.append(pltpu.VMEM((k, tn), BF16))
    in_specs += [pl.BlockSpec((tm, tn), out_map) for _ in extras]
    kern = functools.partial(_ws_kernel, n_lhs=len(lhs), n_w=len(ws), n_extra=len(extras),
                             pairs=pairs, epilogue=epilogue, tm=tm)
    return pl.pallas_call(
        kern,
        grid_spec=pltpu.PrefetchScalarGridSpec(
            num_scalar_prefetch=4, grid=(nn, nm), in_specs=in_specs,
            out_specs=pl.BlockSpec((tm, tn), out_map), scratch_shapes=w_scratch),
        out_shape=jax.ShapeDtypeStruct((rows, n_out), out_dtype),
        compiler_params=_cparams("parallel", "arbitrary"), name=name,
    )(valid, tidx, texp, wnew, *lhs, *[a for a, _ in ws], *extras)


def _kacc_kernel(valid_ref, tidx_ref, texp_ref, x_ref, w_ref, o_ref, *, tm, nk):
    i = pl.program_id(0)
    k = pl.program_id(1)
    n = o_ref.shape[1]
    cn = min(n, KACC_COLS)

    def compute(rows):
        for c in range(n // cn):
            cols = slice(c * cn, (c + 1) * cn)
            part = jnp.dot(x_ref[rows, :], w_ref[:, cols].astype(BF16),
                           preferred_element_type=F32)

            @pl.when(k == 0)
            def _():
                o_ref[rows, cols] = part

            @pl.when(k > 0)
            def _():
                o_ref[rows, cols] += part

    def zero(rows):
        @pl.when(k == 0)
        def _():
            o_ref[rows, :] = jnp.zeros((SUB_ROWS, n), o_ref.dtype)

    _tile_rows(valid_ref[i], tm, compute, zero)


def _matmul_kacc(x, w, lead, *, tm, tk, sched, name):
    valid, tidx, texp, _ = sched
    rows, kdim = x.shape
    n = w.shape[-1]
    nm = valid.shape[0]
    assert kdim % tk == 0
    nk = kdim // tk

    def x_map(i, k, v, t, e):
        return (t[i], jnp.where(v[i] > 0, k, nk - 1))

    def w_map(i, k, v, t, e):
        return tuple(e[i] if s == "e" else s for s in lead) + (jnp.where(v[i] > 0, k, nk - 1), 0)

    def o_map(i, k, v, t, e):
        return (i, 0)

    return pl.pallas_call(
        functools.partial(_kacc_kernel, tm=tm, nk=nk),
        grid_spec=pltpu.PrefetchScalarGridSpec(
            num_scalar_prefetch=3, grid=(nm, nk),
            in_specs=[pl.BlockSpec((tm, tk), x_map),
                      pl.BlockSpec((None,) * len(lead) + (tk, n), w_map)],
            out_specs=pl.BlockSpec((tm, n), o_map)),
        out_shape=jax.ShapeDtypeStruct((rows, n), F32),
        compiler_params=_cparams("arbitrary", "arbitrary"), name=name,
    )(valid, tidx, texp, x, w)


def _dense_sched(rows, tm):
    nm = pl.cdiv(rows, tm)
    valid = [tm] * nm
    valid[-1] = rows - (nm - 1) * tm
    z = jnp.zeros((nm,), I32)
    return (jnp.asarray(valid, I32), jnp.arange(nm, dtype=I32), z, z)


def _ep_first(accs, ex):
    return accs[0]


def _ep_swiglu(accs, ex):
    g, u = accs
    return g * jax.nn.sigmoid(g) * u


def _ep_merge(accs, ex):
    return jax.nn.sigmoid(ex[0]) * accs[0] + jax.nn.sigmoid(ex[1]) * accs[1]


def _softmax_sink(s, sink):
    m = jnp.maximum(jnp.max(s, axis=-1, keepdims=True), sink)
    p = jnp.exp(s - m)
    den = jnp.sum(p, axis=-1, keepdims=True) + jnp.exp(sink - m)
    return p / den


def _swa_kernel(sink_ref, q_ref, kp_ref, kc_ref, vp_ref, vc_ref, o_ref, *, n_off, lead, scale,
                kvh, group, hd, blk):
    n = pl.program_id(1)
    qi = lax.broadcasted_iota(I32, (blk, 2 * blk), 0)
    kj2 = lax.broadcasted_iota(I32, (blk, 2 * blk), 1)
    coord = (n + n_off) * blk + kj2
    window = jnp.logical_and(kj2 > qi, kj2 <= qi + blk)
    mask = jnp.logical_and(coord >= lead, window)
    outs = []
    for g in range(kvh):
        cs = slice(g * hd, (g + 1) * hd)
        kb = jnp.concatenate([kp_ref[:, cs], kc_ref[:, cs]], axis=0).astype(BF16)
        vb = jnp.concatenate([vp_ref[:, cs], vc_ref[:, cs]], axis=0).astype(BF16)
        for h in range(group):
            hh = g * group + h
            qh = q_ref[:, hh * hd:(hh + 1) * hd].astype(BF16)
            s = lax.dot_general(qh, kb, (((1,), (1,)), ((), ())),
                                preferred_element_type=F32) * scale
            s = jnp.where(mask, s, NEG_BIG)
            p = _softmax_sink(s, sink_ref[hh])
            outs.append(jnp.dot(p.astype(BF16), vb, preferred_element_type=F32))
    o_ref[...] = jnp.concatenate(outs, axis=-1).astype(o_ref.dtype)


def _swa_prompt(q_src, k_pad, v_pad, sinks, *, nbatch, nblk, n_off, lead, heads, kvh, hd, blk):
    qw = heads * hd
    kw = kvh * hd
    kern = functools.partial(_swa_kernel, n_off=n_off, lead=lead, scale=hd ** -0.5, kvh=kvh,
                             group=heads // kvh, hd=hd, blk=blk)
    q_spec = pl.BlockSpec((blk, qw), lambda b, n, s: (b * nblk + n, 0))
    prev = pl.BlockSpec((None, blk, kw), lambda b, n, s: (b, n, 0))
    cur = pl.BlockSpec((None, blk, kw), lambda b, n, s: (b, n + 1, 0))
    return pl.pallas_call(
        kern,
        grid_spec=pltpu.PrefetchScalarGridSpec(
            num_scalar_prefetch=1, grid=(nbatch, nblk),
            in_specs=[q_spec, prev, cur, prev, cur],
            out_specs=pl.BlockSpec((blk, qw), lambda b, n, s: (b * nblk + n, 0))),
        out_shape=jax.ShapeDtypeStruct((nbatch * nblk * blk, qw), BF16),
        compiler_params=_cparams("parallel", "parallel"), name="swa_prompt",
    )(sinks, q_src, k_pad, k_pad, v_pad, v_pad)


def _swa_sample_kernel(q_ref, k_ref, v_ref, sink_ref, o_ref, *, scale, kvh, group):
    for g in range(kvh):
        rs = slice(g * group, (g + 1) * group)
        qg = q_ref[rs, :].astype(BF16)
        s = lax.dot_general(qg, k_ref[g].astype(BF16), (((1,), (1,)), ((), ())),
                            preferred_element_type=F32) * scale
        p = _softmax_sink(s, sink_ref[rs, :])
        o_ref[rs, :] = jnp.dot(p.astype(BF16), v_ref[g].astype(BF16),
                               preferred_element_type=F32)


def _swa_sample(q, kc, vc, sinks):
    b, heads, hd = q.shape
    kvh, win = kc.shape[1], kc.shape[2]
    kern = functools.partial(_swa_sample_kernel, scale=hd ** -0.5, kvh=kvh, group=heads // kvh)
    q_spec = pl.BlockSpec((None, heads, hd), lambda i: (i, 0, 0))
    kv_spec = pl.BlockSpec((None, kvh, win, hd), lambda i: (i, 0, 0, 0))
    return pl.pallas_call(
        kern, grid=(b,),
        in_specs=[q_spec, kv_spec, kv_spec, pl.BlockSpec((heads, 1), lambda i: (0, 0))],
        out_specs=q_spec, out_shape=jax.ShapeDtypeStruct((b, heads, hd), F32),
        compiler_params=_cparams("parallel"), name="swa_sample",
    )(q, kc, vc, sinks)


def _softplus(z):
    return jnp.maximum(z, 0.0) + jnp.log1p(jnp.exp(-jnp.abs(z)))


def _rwkv_prep_kernel(r_ref, k_ref, v_ref, lo_ref, rp_ref, kp_ref, vp_ref, lop_ref,
                      mur_ref, muk_ref, muv_ref, mulo_ref, w0_ref, a0_ref, w2_ref, a2_ref,
                      g2_ref, ro_ref, ko_ref, vo_ref, wo_ref, ao_ref, go_ref, *, lora_cols):
    def lerp(u_ref, p_ref, mu_ref):
        u = u_ref[...]
        return u + (p_ref[...] - u) * mu_ref[...]

    ro_ref[...] = lerp(r_ref, rp_ref, mur_ref)
    ko_ref[...] = lerp(k_ref, kp_ref, muk_ref)
    vo_ref[...] = lerp(v_ref, vp_ref, muv_ref)
    lo = lerp(lo_ref, lop_ref, mulo_ref)
    lane = lax.broadcasted_iota(I32, lo.shape, 1)
    lo = jnp.where(lane < lora_cols, lo, 0.0)
    wl = w0_ref[...] + jnp.dot(jnp.tanh(lo), w2_ref[...], precision=HIGHEST,
                               preferred_element_type=F32)
    w_log = -_softplus(-wl) - 0.5
    wo_ref[...] = jnp.exp(-jnp.exp(w_log))
    al = a0_ref[...] + jnp.dot(lo, a2_ref[...], precision=HIGHEST, preferred_element_type=F32)
    ao_ref[...] = jax.nn.sigmoid(al)
    go_ref[...] = jnp.dot(jax.nn.sigmoid(lo), g2_ref[...], precision=HIGHEST,
                          preferred_element_type=F32)


def _rwkv_prep(proj, u_prev, mu, w0, a0, w2p, a2p, g2p, *, u_off, width, lora_cols, cw):
    rows = proj.shape[0]
    tr = _div_tile(rows, 256, SUBLANES)
    nc = width // cw
    ob = u_off // cw
    wb = width // cw

    def blk(off):
        return pl.BlockSpec((tr, cw), lambda i, c: (i, off + c))

    def blk_fixed(off):
        return pl.BlockSpec((tr, cw), lambda i, c: (i, off))

    def vec(off):
        return pl.BlockSpec((1, cw), lambda i, c: (0, off + c))

    def vec_fixed(off):
        return pl.BlockSpec((1, cw), lambda i, c: (0, off))

    wspec = pl.BlockSpec((cw, cw), lambda i, c: (0, c))
    in_specs = [blk(ob), blk(ob + wb), blk(ob + 2 * wb), blk_fixed(ob + 3 * wb),
                blk(0), blk(wb), blk(2 * wb), blk_fixed(3 * wb),
                vec(0), vec(wb), vec(2 * wb), vec_fixed(3 * wb),
                vec(0), vec(0), wspec, wspec, wspec]
    out = jax.ShapeDtypeStruct((rows, width), F32)
    mu2 = mu.reshape(1, -1)
    return pl.pallas_call(
        functools.partial(_rwkv_prep_kernel, lora_cols=lora_cols),
        grid=(rows // tr, nc), in_specs=in_specs, out_specs=[blk(0)] * 6, out_shape=[out] * 6,
        compiler_params=_cparams("parallel", "parallel"), name="rwkv_prep",
    )(proj, proj, proj, proj, u_prev, u_prev, u_prev, u_prev, mu2, mu2, mu2, mu2,
      w0.reshape(1, -1), a0.reshape(1, -1), w2p, a2p, g2p)


def _rwkv_scan_kernel(r_ref, w_ref, k_ref, v_ref, a_ref, kk_ref, ka_ref, rk_ref, gw_ref, gb_ref,
                      s0_ref, y_ref, sfin_ref, state, a_s, b_s, k_s, wr_s, *, tc, nt, hd):
    tci = pl.program_id(1)

    @pl.when(tci == 0)
    def _():
        state[...] = s0_ref[...]

    sub = lax.broadcasted_iota(I32, (SUBLANES, LANES), 0)

    def step(t, carry):
        r = r_ref[t]
        w = w_ref[t]
        k = k_ref[t]
        v = v_ref[t]
        a = a_ref[t]
        kk = k * kk_ref[...]
        nrm = jnp.sqrt(jnp.sum(kk * kk, axis=0, keepdims=True))
        kk = kk / jnp.maximum(nrm, 1e-12)
        k2 = k * (1.0 + (a - 1.0) * ka_ref[...])
        bt = kk * a
        a_s[...] = -kk
        b_s[...] = bt
        k_s[...] = k2
        wr_s[...] = w * r
        rb = jnp.sum(bt * r, axis=0, keepdims=True)
        rk = jnp.sum(k2 * r, axis=0, keepdims=True)
        bonus = jnp.sum(r * k2 * rk_ref[...], axis=0, keepdims=True)

        def vgroup(gi, c2):
            base = pl.multiple_of(gi * SUBLANES, SUBLANES)
            vt = v_ref[t, pl.ds(base, SUBLANES), :]
            yt = jnp.zeros((SUBLANES, LANES), F32)
            for j in range(SUBLANES):
                sv = state[base + j]
                sa = jnp.sum(sv * a_s[...], axis=0, keepdims=True)
                y0 = jnp.sum(sv * wr_s[...], axis=0, keepdims=True)
                vv = vt[j:j + 1, :]
                state[base + j] = sv * w_ref[t] + sa * b_s[...] + vv * k_s[...]
                yt = jnp.where(sub == j, y0 + sa * rb + vv * rk, yt)
            y_ref[t, pl.ds(base, SUBLANES), :] = yt
            return c2

        lax.fori_loop(0, hd // SUBLANES, vgroup, 0)
        y = y_ref[t]
        mu = jnp.mean(y, axis=0, keepdims=True)
        yc = y - mu
        var = jnp.mean(yc * yc, axis=0, keepdims=True)
        y_ref[t] = yc * lax.rsqrt(var + GN_EPS) * gw_ref[...] + gb_ref[...] + bonus * v
        return carry

    lax.fori_loop(0, tc, step, 0)

    @pl.when(tci == nt - 1)
    def _():
        sfin_ref[...] = state[...]


def _rwkv_scan(r, w, k, v, a, params, s0):
    t, hd, lanes = r.shape
    tc = _div_tile(t, 48, 1)
    nt = t // tc
    step_spec = pl.BlockSpec((tc, hd, LANES), lambda g, c: (c, 0, g))
    par_spec = pl.BlockSpec((hd, LANES), lambda g, c: (0, g))
    st_spec = pl.BlockSpec((hd, hd, LANES), lambda g, c: (0, 0, g))
    tile = pltpu.VMEM((hd, LANES), F32)
    return pl.pallas_call(
        functools.partial(_rwkv_scan_kernel, tc=tc, nt=nt, hd=hd),
        grid=(lanes // LANES, nt),
        in_specs=[step_spec] * 5 + [par_spec] * 5 + [st_spec],
        out_specs=[step_spec, st_spec],
        out_shape=[jax.ShapeDtypeStruct((t, hd, lanes), F32),
                   jax.ShapeDtypeStruct((hd, hd, lanes), F32)],
        scratch_shapes=[pltpu.VMEM((hd, hd, LANES), F32), tile, tile, tile, tile],
        compiler_params=_cparams("parallel", "arbitrary"), name="rwkv_scan",
    )(r, w, k, v, a, *params, s0)


def _mul_kernel(a_ref, b_ref, o_ref):
    o_ref[...] = (a_ref[...] * b_ref[...]).astype(o_ref.dtype)


def _mul_bf16(a, b):
    r, d = a.shape
    tr = _div_tile(r, 512, SUBLANES)
    spec = pl.BlockSpec((tr, d), lambda i: (i, 0))
    return pl.pallas_call(_mul_kernel, grid=(r // tr,), in_specs=[spec, spec], out_specs=spec,
                          out_shape=jax.ShapeDtypeStruct((r, d), BF16),
                          compiler_params=_cparams("parallel"), name="gate_mul")(a, b)


def _router_kernel(x_ref, w_ref, idx_ref, gate_ref, *, n_exp):
    logits = jnp.dot(x_ref[...], w_ref[...], precision=HIGHEST, preferred_element_type=F32)
    lane = lax.broadcasted_iota(I32, logits.shape, 1)
    lg = jnp.where(lane < n_exp, logits, -jnp.inf)
    m1 = jnp.max(lg, axis=-1, keepdims=True)
    i1 = jnp.min(jnp.where(lg == m1, lane, LANES), axis=-1, keepdims=True)
    lg2 = jnp.where(lane == i1, -jnp.inf, lg)
    m2 = jnp.max(lg2, axis=-1, keepdims=True)
    i2 = jnp.min(jnp.where(lg2 == m2, lane, LANES), axis=-1, keepdims=True)
    e = jnp.exp(m2 - m1)
    g1 = 1.0 / (1.0 + e)
    g2 = e / (1.0 + e)
    idx_ref[...] = jnp.where(lane == 0, i1, jnp.where(lane == 1, i2, 0))
    gate_ref[...] = jnp.where(lane == 0, g1, jnp.where(lane == 1, g2, 0.0))


def _router(x, w_router):
    r, d = x.shape
    n_exp = w_router.shape[1]
    wp = jnp.zeros((d, LANES), F32).at[:, :n_exp].set(w_router)
    tr = _div_tile(r, 512, SUBLANES)
    row = pl.BlockSpec((tr, d), lambda i: (i, 0))
    out = pl.BlockSpec((tr, LANES), lambda i: (i, 0))
    return pl.pallas_call(
        functools.partial(_router_kernel, n_exp=n_exp), grid=(r // tr,),
        in_specs=[row, pl.BlockSpec((d, LANES), lambda i: (0, 0))], out_specs=(out, out),
        out_shape=(jax.ShapeDtypeStruct((r, LANES), I32), jax.ShapeDtypeStruct((r, LANES), F32)),
        compiler_params=_cparams("parallel"), name="router",
    )(x, wp)


def _row_copy(src_hbm, row, dst, r, sem):
    return pltpu.make_async_copy(src_hbm.at[pl.ds(row, 1), :], dst.at[pl.ds(r, 1), :], sem)


def _dispatch_kernel(src_ref, nused_ref, x_hbm, o_ref, buf, sem, *, tg):
    base = pl.program_id(0) * tg

    @pl.when(base < nused_ref[0])
    def _():
        def issue(r, c):
            _row_copy(x_hbm, src_ref[base + r], buf, r, sem).start()
            return c

        def wait(r, c):
            _row_copy(x_hbm, 0, buf, r, sem).wait()
            return c

        lax.fori_loop(0, tg, issue, 0)
        lax.fori_loop(0, tg, wait, 0)
        o_ref[...] = buf[...].astype(BF16)

    @pl.when(base >= nused_ref[0])
    def _():
        o_ref[...] = jnp.zeros_like(o_ref)


def _dispatch(x, src, nused, *, tg):
    p = src.shape[0]
    d = x.shape[1]
    return pl.pallas_call(
        functools.partial(_dispatch_kernel, tg=tg),
        grid_spec=pltpu.PrefetchScalarGridSpec(
            num_scalar_prefetch=2, grid=(p // tg,),
            in_specs=[pl.BlockSpec(memory_space=pl.ANY)],
            out_specs=pl.BlockSpec((tg, d), lambda g, s, n: (g, 0)),
            scratch_shapes=[pltpu.VMEM((tg, d), F32), pltpu.SemaphoreType.DMA(())]),
        out_shape=jax.ShapeDtypeStruct((p, d), BF16),
        compiler_params=_cparams("arbitrary"), name="moe_dispatch",
    )(src, nused, x)


def _combine_kernel(d1_ref, d2_ref, o_hbm, gate_ref, x_ref, g_ref, b_ref, xo_ref, xb_ref,
                    buf1, buf2, sem1, sem2, *, tg, alpha):
    base = pl.program_id(0) * tg

    def issue(r, c):
        _row_copy(o_hbm, d1_ref[base + r], buf1, r, sem1).start()
        _row_copy(o_hbm, d2_ref[base + r], buf2, r, sem2).start()
        return c

    def wait(r, c):
        _row_copy(o_hbm, 0, buf1, r, sem1).wait()
        _row_copy(o_hbm, 0, buf2, r, sem2).wait()
        return c

    lax.fori_loop(0, tg, issue, 0)
    lax.fori_loop(0, tg, wait, 0)
    gates = gate_ref[...]
    y = gates[:, 0:1] * buf1[...] + gates[:, 1:2] * buf2[...]
    out = _ln_rows(alpha * x_ref[...] + y, g_ref[...], b_ref[...])
    xo_ref[...] = out
    xb_ref[...] = out.astype(BF16)


def _combine_ln(o_sorted, d1, d2, gates, x, g, b, *, alpha, tg):
    r, d = x.shape
    row = pl.BlockSpec((tg, d), lambda i, a, c: (i, 0))
    vec = pl.BlockSpec((1, d), lambda i, a, c: (0, 0))
    return pl.pallas_call(
        functools.partial(_combine_kernel, tg=tg, alpha=alpha),
        grid_spec=pltpu.PrefetchScalarGridSpec(
            num_scalar_prefetch=2, grid=(r // tg,),
            in_specs=[pl.BlockSpec(memory_space=pl.ANY),
                      pl.BlockSpec((tg, LANES), lambda i, a, c: (i, 0)), row, vec, vec],
            out_specs=(row, row),
            scratch_shapes=[pltpu.VMEM((tg, d), F32), pltpu.VMEM((tg, d), F32),
                            pltpu.SemaphoreType.DMA(()), pltpu.SemaphoreType.DMA(())]),
        out_shape=(jax.ShapeDtypeStruct((r, d), F32), jax.ShapeDtypeStruct((r, d), BF16)),
        compiler_params=_cparams("arbitrary"), name="moe_combine_ln",
    )(d1, d2, o_sorted, gates, x, g.reshape(1, d), b.reshape(1, d))


def _route_plan(idx, n_exp, tm):
    r = idx.shape[0]
    flat = idx.reshape(-1)
    onehot = (flat[:, None] == jnp.arange(n_exp, dtype=I32)[None, :]).astype(I32)
    rank = jnp.sum((jnp.cumsum(onehot, axis=0) - onehot) * onehot, axis=1)
    sizes = jnp.sum(onehot, axis=0)
    etiles = (sizes + tm - 1) // tm
    tend = jnp.cumsum(etiles)
    tstart = tend - etiles
    dest = tstart[flat] * tm + rank
    nt = pl.cdiv(2 * r, tm) + n_exp
    tid = jnp.arange(nt, dtype=I32)
    n_used = tend[-1]
    te = jnp.minimum(jnp.sum((tid[:, None] >= tend[None, :]).astype(I32), axis=1), n_exp - 1)
    used = tid < n_used
    last = jnp.maximum(n_used - 1, 0)
    tidx = jnp.minimum(tid, last)
    texp = te[tidx]
    valid = jnp.where(used, jnp.clip(sizes[texp] - (tid - tstart[texp]) * tm, 0, tm), 0)
    prev = jnp.concatenate([jnp.full((1,), -1, I32), texp[:-1]])
    wnew = jnp.logical_and(used, texp != prev).astype(I32)
    src = jnp.zeros((nt * tm,), I32).at[dest].set(jnp.arange(2 * r, dtype=I32) // 2)
    sched = (valid.astype(I32), tidx.astype(I32), texp.astype(I32), wnew)
    dest2 = dest.reshape(r, 2).astype(I32)
    return sched, src, (n_used * tm).reshape(1).astype(I32), dest2[:, 0], dest2[:, 1]


def kernel(x_prompt, x_sample, cache_swa_k, cache_swa_v, state_rwkv, state_shift, meta_tokens, ln_emb_g, ln_emb_b, w_in, att_sinks, rwkv_mu, rwkv_w0, rwkv_w2, rwkv_a0, rwkv_a2, rwkv_g2, rwkv_k_k, rwkv_k_a, rwkv_r_k, rwkv_gn_w, rwkv_gn_b, w_proj_att, w_proj_rwkv, w_out, ln_mix_g, ln_mix_b, ln_ffn_g, ln_ffn_b, ffn_w_gate, ffn_w_up, ffn_w_down, moe_router, moe_w_gate, moe_w_up, moe_w_down):
    nb, seq, dm = x_prompt.shape
    db, dseq, _ = x_sample.shape
    assert dseq == 1
    depth = w_in.shape[0]
    n_meta = meta_tokens.shape[0]
    _, _, window, kvh, hd = cache_swa_k.shape
    heads = att_sinks.shape[1]
    rh = state_rwkv.shape[2]
    att_w = heads * hd
    kv_w = kvh * hd
    rw = rh * hd
    shift_cols = state_shift.shape[2]
    lora_cols = shift_cols - 3 * rw
    d_lora, i_lora, g_lora = rwkv_w2.shape[1], rwkv_a2.shape[1], rwkv_g2.shape[1]
    assert d_lora + i_lora + g_lora == lora_cols
    n_exp = moe_router.shape[2]
    blk = window
    assert seq % blk == 0 and n_meta <= blk and (nb * rh) % LANES == 0 and (db * rh) % LANES == 0
    nblk = seq // blk
    lead = blk - n_meta
    alpha = (2 * depth) ** 0.25

    nr = nb * seq
    m0, s0r = nr, nr + n_meta
    rows = nr + _round_up(n_meta + db, 64)
    u_off = att_w + 2 * kv_w
    g_off = u_off + shift_cols
    cw = 512
    assert u_off % cw == 0 and rw % cw == 0 and lora_cols <= cw

    tm = 1024
    sched = _dense_sched(rows, tm)

    x0 = jnp.concatenate([x_prompt.reshape(nr, dm), meta_tokens, x_sample.reshape(db, dm),
                          jnp.zeros((rows - s0r - db, dm), F32)], axis=0)
    x, xb = _layer_norm(x0, ln_emb_g, ln_emb_b)

    def to_lanes_prompt(z):
        real = z[:nr].reshape(nb, seq, rh, hd)
        meta = jnp.broadcast_to(z[m0:m0 + n_meta].reshape(1, n_meta, rh, hd), (nb, n_meta, rh, hd))
        full = jnp.concatenate([meta, real], axis=1)
        return full.transpose(1, 3, 0, 2).reshape(n_meta + seq, hd, nb * rh)

    def to_lanes_sample(z):
        return z[s0r:s0r + db].reshape(db, rh, hd).transpose(2, 0, 1).reshape(1, hd, db * rh)

    def head_param(p, nbatch):
        return jnp.broadcast_to(p.reshape(rh, hd).T[:, None, :], (hd, nbatch, rh)).reshape(hd, nbatch * rh)

    outs = {k: [] for k in ("pk", "pv", "ps", "psh", "sk", "sv", "ss", "ssh")}
    for l in range(depth):
        proj = _matmul_ws([xb], [(w_in, (l,))], [(0, 0)], [], _ep_first, F32, w_in.shape[2],
                          tm=tm, tn=512, sched=sched, name="proj_in")

        k_all = proj[:, att_w:att_w + kv_w]
        v_all = proj[:, att_w + kv_w:att_w + 2 * kv_w]
        sinks = att_sinks[l]

        def padded(z):
            lead_blk = jnp.concatenate([jnp.zeros((lead, kv_w), F32), z[m0:m0 + n_meta]], axis=0)
            real = z[:nr].reshape(nb, seq, kv_w)
            full = jnp.concatenate([jnp.broadcast_to(lead_blk[None], (nb, blk, kv_w)), real], axis=1)
            meta_only = jnp.concatenate([jnp.zeros((blk, kv_w), F32), lead_blk], axis=0)[None]
            return full, meta_only

        k_pad, k_meta = padded(k_all)
        v_pad, v_meta = padded(v_all)
        swa = functools.partial(_swa_prompt, lead=lead, heads=heads, kvh=kvh, hd=hd, blk=blk)
        o_real = swa(proj, k_pad, v_pad, sinks, nbatch=nb, nblk=nblk, n_off=0)
        q_meta = jnp.concatenate([jnp.zeros((lead, att_w), F32), proj[m0:m0 + n_meta, :att_w]], axis=0)
        o_meta = swa(q_meta, k_meta, v_meta, sinks, nbatch=1, nblk=1, n_off=-1)[lead:]

        k_new = k_all[s0r:s0r + db].reshape(db, 1, kvh, hd)
        v_new = v_all[s0r:s0r + db].reshape(db, 1, kvh, hd)
        kc = jnp.concatenate([cache_swa_k[l][:, 1:], k_new], axis=1)
        vc = jnp.concatenate([cache_swa_v[l][:, 1:], v_new], axis=1)
        q_s = proj[s0r:s0r + db, :att_w].reshape(db, heads, hd)
        o_s = _swa_sample(q_s, kc.transpose(0, 2, 1, 3), vc.transpose(0, 2, 1, 3),
                          sinks.reshape(heads, 1))
        y_att = jnp.concatenate([o_real, o_meta, o_s.reshape(db, att_w).astype(BF16),
                                 jnp.zeros((rows - s0r - db, att_w), BF16)], axis=0)
        outs["pk"].append(k_all[:nr].reshape(nb, seq, kvh, hd)[:, seq - window:])
        outs["pv"].append(v_all[:nr].reshape(nb, seq, kvh, hd)[:, seq - window:])
        outs["sk"].append(kc)
        outs["sv"].append(vc)

        u = proj[:, u_off:u_off + shift_cols]
        u_real = u[:nr].reshape(nb, seq, shift_cols)
        u_meta = u[m0:m0 + n_meta]
        prev_real = jnp.concatenate(
            [jnp.broadcast_to(u_meta[n_meta - 1][None, None], (nb, 1, shift_cols)), u_real[:, :-1]], axis=1)
        prev_meta = jnp.concatenate([jnp.zeros((1, shift_cols), F32), u_meta[:-1]], axis=0)
        u_prev = jnp.concatenate([prev_real.reshape(nr, shift_cols), prev_meta, state_shift[l],
                                  jnp.zeros((rows - s0r - db, shift_cols), F32)], axis=0)
        w2p = jnp.zeros((cw, rw), F32).at[:d_lora].set(rwkv_w2[l])
        a2p = jnp.zeros((cw, rw), F32).at[d_lora:d_lora + i_lora].set(rwkv_a2[l])
        g2p = jnp.zeros((cw, rw), F32).at[d_lora + i_lora:lora_cols].set(rwkv_g2[l])
        r_n, k_n, v_n, w_n, a_n, g_n = _rwkv_prep(
            proj, u_prev, rwkv_mu[l], rwkv_w0[l], rwkv_a0[l], w2p, a2p, g2p,
            u_off=u_off, width=rw, lora_cols=lora_cols, cw=cw)

        hp = (rwkv_k_k[l], rwkv_k_a[l], rwkv_r_k[l], rwkv_gn_w[l], rwkv_gn_b[l])
        y_p, s_p = _rwkv_scan(*[to_lanes_prompt(z) for z in (r_n, w_n, k_n, v_n, a_n)],
                              [head_param(p, nb) for p in hp],
                              jnp.zeros((hd, hd, nb * rh), F32))
        y_s, s_s = _rwkv_scan(*[to_lanes_sample(z) for z in (r_n, w_n, k_n, v_n, a_n)],
                              [head_param(p, db) for p in hp],
                              state_rwkv[l].transpose(2, 3, 0, 1).reshape(hd, hd, db * rh))
        y_p = y_p.reshape(n_meta + seq, hd, nb, rh).transpose(2, 0, 3, 1)
        y_rwkv = jnp.concatenate([
            y_p[:, n_meta:].reshape(nr, rw), y_p[0, :n_meta].reshape(n_meta, rw),
            y_s.reshape(hd, db, rh).transpose(1, 2, 0).reshape(db, rw),
            jnp.zeros((rows - s0r - db, rw), F32)], axis=0)
        yg = _mul_bf16(y_rwkv, g_n)
        outs["ps"].append(s_p.reshape(hd, hd, nb, rh).transpose(2, 3, 0, 1))
        outs["ss"].append(s_s.reshape(hd, hd, db, rh).transpose(2, 3, 0, 1))
        outs["psh"].append(u_real[:, seq - 1])
        outs["ssh"].append(u[s0r:s0r + db])

        gate_att = proj[:, g_off:g_off + dm]
        gate_rwkv = proj[:, g_off + dm:g_off + 2 * dm]
        merged = _matmul_ws([y_att, yg], [(w_proj_att, (l,)), (w_proj_rwkv, (l,))],
                            [(0, 0), (1, 1)], [gate_att, gate_rwkv], _ep_merge, BF16, dm,
                            tm=tm, tn=512, sched=sched, name="merge")
        mix = _matmul_ws([merged], [(w_out, (l,))], [(0, 0)], [], _ep_first, F32, dm,
                         tm=tm, tn=512, sched=sched, name="proj_out")
        x, xb = _layer_norm(x, ln_mix_g[l], ln_mix_b[l], y=mix, alpha=alpha)

        i = l // 2
        if l % 2 == 0:
            h = _matmul_ws([xb], [(ffn_w_gate, (i,)), (ffn_w_up, (i,))], [(0, 0), (0, 1)], [],
                           _ep_swiglu, BF16, ffn_w_gate.shape[2], tm=tm, tn=256, sched=sched,
                           name="ffn_gate_up")
            y = _matmul_kacc(h, ffn_w_down, (i,), tm=tm, tk=256, sched=sched, name="ffn_down")
            x, xb = _layer_norm(x, ln_ffn_g[l], ln_ffn_b[l], y=y, alpha=alpha)
        else:
            idx, gates = _router(x, moe_router[i])
            msched, src, nused, d1, d2 = _route_plan(idx[:, :2], n_exp, tm)
            xs = _dispatch(x, src, nused, tg=256)
            h = _matmul_ws([xs], [(moe_w_gate, (i, "e")), (moe_w_up, (i, "e"))],
                           [(0, 0), (0, 1)], [], _ep_swiglu, BF16, moe_w_gate.shape[3],
                           tm=tm, tn=256, sched=msched, name="moe_gate_up")
            o = _matmul_kacc(h, moe_w_down, (i, "e"), tm=tm, tk=256, sched=msched, name="moe_down")
            x, xb = _combine_ln(o, d1, d2, gates, x, ln_ffn_g[l], ln_ffn_b[l], alpha=alpha,
                                tg=_div_tile(rows, 128, SUBLANES))

    y_prompt = x[:nr].reshape(nb, seq, dm)
    y_sample = x[s0r:s0r + db].reshape(db, 1, dm)
    st = {k: jnp.stack(v) for k, v in outs.items()}
    return (y_prompt, y_sample, st["pk"], st["pv"], st["ps"], st["psh"],
            st["sk"], st["sv"], st["ss"], st["ssh"])
```

```python
import functools

import jax
import jax.numpy as jnp
from jax import lax
from jax.experimental import pallas as pl
from jax.experimental.pallas import tpu as pltpu

F32 = jnp.float32
BF16 = jnp.bfloat16
I32 = jnp.int32

LANES = 128
SUBLANES = 8
VMEM_LIMIT = 56 * 1024 * 1024
LN_EPS = 1e-5
GN_EPS = 64e-5
NEG_BIG = -1e30
HIGHEST = lax.Precision.HIGHEST


def _cparams(*sem):
    return pltpu.CompilerParams(dimension_semantics=sem, vmem_limit_bytes=VMEM_LIMIT)


def _round_up(x, m):
    return (x + m - 1) // m * m


def _div_tile(n, target, mult):
    best = None
    for t in range(mult, min(n, target) + 1, mult):
        if n % t == 0:
            best = t
    assert best is not None, (n, target, mult)
    return best


def _ln_rows(x, g, b):
    mu = jnp.mean(x, axis=-1, keepdims=True)
    xc = x - mu
    var = jnp.mean(xc * xc, axis=-1, keepdims=True)
    return xc * lax.rsqrt(var + LN_EPS) * g + b


def _embed_ln_kernel(x_ref, t_ref, g_ref, b_ref, o_ref, ob_ref, *, n_real_tiles):
    x = jnp.where(pl.program_id(0) == n_real_tiles, t_ref[...], x_ref[...])
    y = _ln_rows(x, g_ref[...], b_ref[...])
    o_ref[...] = y
    ob_ref[...] = y.astype(BF16)


def _embed_ln(x_real, x_tail, g, b, rows, tr):
    n_real, d = x_real.shape
    assert n_real % tr == 0 and x_tail.shape == (tr, d) and 0 < rows - n_real <= tr
    nrt = n_real // tr
    vec = pl.BlockSpec((1, d), lambda i: (0, 0))
    row = pl.BlockSpec((tr, d), lambda i: (i, 0))
    return pl.pallas_call(
        functools.partial(_embed_ln_kernel, n_real_tiles=nrt), grid=(nrt + 1,),
        in_specs=[pl.BlockSpec((tr, d), lambda i: (jnp.minimum(i, nrt - 1), 0)),
                  pl.BlockSpec((tr, d), lambda i: (0, 0)), vec, vec],
        out_specs=(row, row),
        out_shape=(jax.ShapeDtypeStruct((rows, d), F32), jax.ShapeDtypeStruct((rows, d), BF16)),
        compiler_params=_cparams("arbitrary"), name="embed_ln",
    )(x_real, x_tail, g.reshape(1, d), b.reshape(1, d))


def _res_ln_kernel(x_ref, y_ref, g_ref, b_ref, o_ref, ob_ref, *, alpha):
    y = _ln_rows(alpha * x_ref[...] + y_ref[...], g_ref[...], b_ref[...])
    o_ref[...] = y
    ob_ref[...] = y.astype(BF16)


def _res_layer_norm(x, y, g, b, alpha):
    r, d = x.shape
    tr = _div_tile(r, 256, SUBLANES)
    row = pl.BlockSpec((tr, d), lambda i: (i, 0))
    vec = pl.BlockSpec((1, d), lambda i: (0, 0))
    out_shape = (jax.ShapeDtypeStruct((r, d), F32), jax.ShapeDtypeStruct((r, d), BF16))
    return pl.pallas_call(
        functools.partial(_res_ln_kernel, alpha=alpha), grid=(r // tr,),
        in_specs=[row, row, vec, vec], out_specs=(row, row), out_shape=out_shape,
        compiler_params=_cparams("parallel"), name="res_ln",
    )(x, y, g.reshape(1, d), b.reshape(1, d))


SUB_ROWS = 256
KACC_COLS = 1024


def _tile_rows(valid, tm, fn, zero_fn):
    @pl.when(valid == tm)
    def _():
        fn(slice(None))

    @pl.when(valid < tm)
    def _():
        for s in range(tm // SUB_ROWS):
            rows = pl.ds(s * SUB_ROWS, SUB_ROWS)

            @pl.when(s * SUB_ROWS < valid)
            def _():
                fn(rows)

            @pl.when(s * SUB_ROWS >= valid)
            def _():
                zero_fn(rows)


def _ws_kernel(valid_ref, tidx_ref, texp_ref, wnew_ref, *refs, n_lhs, n_w, n_extra, pairs,
               epilogue, tm, tn, n_real):
    lhs = refs[:n_lhs]
    ws = refs[n_lhs:n_lhs + n_w]
    extras = refs[n_lhs + n_w:n_lhs + n_w + n_extra]
    o_ref = refs[n_lhs + n_w + n_extra]
    wb = refs[n_lhs + n_w + n_extra + 1:]
    j = pl.program_id(0)
    i = pl.program_id(1)

    @pl.when(jnp.logical_or(i == 0, wnew_ref[i] == 1))
    def _():
        for w_ref, wb_ref in zip(ws, wb):
            wb_ref[...] = w_ref[...].astype(BF16)

    def compute(rows):
        accs = [jnp.dot(lhs[li][rows, :], wb[wi][...], preferred_element_type=F32)
                for li, wi in pairs]
        ex = [e[rows, :] for e in extras]
        res = epilogue(accs, ex)
        if n_real is not None:
            col = lax.broadcasted_iota(I32, res.shape, 1)
            res = jnp.where(col < n_real - j * tn, res, 0.0)
        o_ref[rows, :] = res.astype(o_ref.dtype)

    def zero(rows):
        o_ref[rows, :] = jnp.zeros((SUB_ROWS, o_ref.shape[1]), o_ref.dtype)

    _tile_rows(valid_ref[i], tm, compute, zero)


def _matmul_ws(lhs, ws, pairs, extras, epilogue, out_dtype, n_out, *, tm, tn, sched, name):
    valid, tidx, texp, wnew = sched
    rows, _ = lhs[0].shape
    nm = valid.shape[0]
    tn = min(tn, n_out)
    nn = pl.cdiv(n_out, tn)
    w_cols = ws[0][0].shape[-1]
    last_w_blk = pl.cdiv(w_cols, tn) - 1
    n_real = w_cols if n_out > w_cols else None

    def lhs_map(j, i, v, t, e, wn):
        return (t[i], 0)

    def w_map(lead):
        def m(j, i, v, t, e, wn):
            return tuple(e[i] if s == "e" else s for s in lead) + (0, jnp.minimum(j, last_w_blk))
        return m

    def out_map(j, i, v, t, e, wn):
        return (i, j)

    def extra_map(off):
        return lambda j, i, v, t, e, wn: (i, j + off)

    in_specs = [pl.BlockSpec((tm, x.shape[1]), lhs_map) for x in lhs]
    w_scratch = []
    for arr, lead in ws:
        k = arr.shape[-2]
        in_specs.append(pl.BlockSpec((None,) * len(lead) + (k, tn), w_map(lead)))
        w_scratch.append(pltpu.VMEM((k, tn), BF16))
    in_specs += [pl.BlockSpec((tm, tn), extra_map(off)) for _, off in extras]
    kern = functools.partial(_ws_kernel, n_lhs=len(lhs), n_w=len(ws), n_extra=len(extras),
                             pairs=pairs, epilogue=epilogue, tm=tm, tn=tn, n_real=n_real)
    return pl.pallas_call(
        kern,
        grid_spec=pltpu.PrefetchScalarGridSpec(
            num_scalar_prefetch=4, grid=(nn, nm), in_specs=in_specs,
            out_specs=pl.BlockSpec((tm, tn), out_map), scratch_shapes=w_scratch),
        out_shape=jax.ShapeDtypeStruct((rows, n_out), out_dtype),
        compiler_params=_cparams("parallel", "arbitrary"), name=name,
    )(valid, tidx, texp, wnew, *lhs, *[a for a, _ in ws], *[a for a, _ in extras])


def _kacc_kernel(valid_ref, tidx_ref, texp_ref, x_ref, w_ref, o_ref, *, tm, tk, k_real, k_pad):
    i = pl.program_id(0)
    k = pl.program_id(2)
    n = o_ref.shape[1]
    cn = min(n, KACC_COLS)

    def compute(rows):
        for c in range(n // cn):
            cols = slice(c * cn, (c + 1) * cn)
            w = w_ref[:, cols]
            if k_pad != k_real:
                row = lax.broadcasted_iota(I32, w.shape, 0)
                w = jnp.where(row < k_real - k * tk, w, 0.0)
            part = jnp.dot(x_ref[rows, :], w.astype(BF16), preferred_element_type=F32)

            @pl.when(k == 0)
            def _():
                o_ref[rows, cols] = part

            @pl.when(k > 0)
            def _():
                o_ref[rows, cols] += part

    def zero(rows):
        @pl.when(k == 0)
        def _():
            o_ref[rows, :] = jnp.zeros((SUB_ROWS, n), o_ref.dtype)

    _tile_rows(valid_ref[i], tm, compute, zero)


def _matmul_kacc(x, w, lead, *, tm, tn, tk, sched, name):
    valid, tidx, texp, _ = sched
    rows, k_pad = x.shape
    k_real, n = w.shape[-2:]
    nm = valid.shape[0]
    tn = min(tn, n)
    tk = min(tk, k_pad)
    assert k_pad % tk == 0 and n % tn == 0 and k_pad - k_real < tk
    nk = k_pad // tk

    def x_map(i, j, k, v, t, e):
        return (t[i], jnp.where(v[i] > 0, k, nk - 1))

    def w_map(i, j, k, v, t, e):
        return tuple(e[i] if s == "e" else s for s in lead) + (jnp.where(v[i] > 0, k, nk - 1), j)

    def o_map(i, j, k, v, t, e):
        return (i, j)

    return pl.pallas_call(
        functools.partial(_kacc_kernel, tm=tm, tk=tk, k_real=k_real, k_pad=k_pad),
        grid_spec=pltpu.PrefetchScalarGridSpec(
            num_scalar_prefetch=3, grid=(nm, n // tn, nk),
            in_specs=[pl.BlockSpec((tm, tk), x_map),
                      pl.BlockSpec((None,) * len(lead) + (tk, tn), w_map)],
            out_specs=pl.BlockSpec((tm, tn), o_map)),
        out_shape=jax.ShapeDtypeStruct((rows, n), F32),
        compiler_params=_cparams("arbitrary", "arbitrary", "arbitrary"), name=name,
    )(valid, tidx, texp, x, w)


def _dense_sched(rows, tm):
    nm = pl.cdiv(rows, tm)
    valid = [tm] * nm
    valid[-1] = rows - (nm - 1) * tm
    z = jnp.zeros((nm,), I32)
    return (jnp.asarray(valid, I32), jnp.arange(nm, dtype=I32), z, z)


def _ep_first(accs, ex):
    return accs[0]


def _ep_swiglu(accs, ex):
    g, u = accs
    return g * jax.nn.sigmoid(g) * u


def _ep_merge(accs, ex):
    return jax.nn.sigmoid(ex[0]) * accs[0] + jax.nn.sigmoid(ex[1]) * accs[1]


def _softmax_sink(s, sink):
    m = jnp.maximum(jnp.max(s, axis=-1, keepdims=True), sink)
    p = jnp.exp(s - m)
    den = jnp.sum(p, axis=-1, keepdims=True) + jnp.exp(sink - m)
    return p / den


def _swa_kernel(sink_ref, q_ref, kp_ref, kc_ref, vp_ref, vc_ref, o_ref, *, n_off, lead, scale,
                kvh, group, hd, blk):
    n = pl.program_id(1)
    qi = lax.broadcasted_iota(I32, (blk, 2 * blk), 0)
    kj2 = lax.broadcasted_iota(I32, (blk, 2 * blk), 1)
    coord = (n + n_off) * blk + kj2
    window = jnp.logical_and(kj2 > qi, kj2 <= qi + blk)
    mask = jnp.logical_and(coord >= lead, window)
    outs = []
    for g in range(kvh):
        cs = slice(g * hd, (g + 1) * hd)
        kb = jnp.concatenate([kp_ref[:, cs], kc_ref[:, cs]], axis=0).astype(BF16)
        vb = jnp.concatenate([vp_ref[:, cs], vc_ref[:, cs]], axis=0).astype(BF16)
        for h in range(group):
            hh = g * group + h
            qh = q_ref[:, hh * hd:(hh + 1) * hd].astype(BF16)
            s = lax.dot_general(qh, kb, (((1,), (1,)), ((), ())),
                                preferred_element_type=F32) * scale
            s = jnp.where(mask, s, NEG_BIG)
            p = _softmax_sink(s, sink_ref[hh])
            outs.append(jnp.dot(p.astype(BF16), vb, preferred_element_type=F32))
    o_ref[...] = jnp.concatenate(outs, axis=-1).astype(o_ref.dtype)


def _swa_prompt(q_src, k_pad, v_pad, sinks, *, nbatch, nblk, n_off, lead, heads, kvh, hd, blk):
    qw = heads * hd
    kw = kvh * hd
    kern = functools.partial(_swa_kernel, n_off=n_off, lead=lead, scale=hd ** -0.5, kvh=kvh,
                             group=heads // kvh, hd=hd, blk=blk)
    q_spec = pl.BlockSpec((blk, qw), lambda b, n, s: (b * nblk + n, 0))
    prev = pl.BlockSpec((None, blk, kw), lambda b, n, s: (b, n, 0))
    cur = pl.BlockSpec((None, blk, kw), lambda b, n, s: (b, n + 1, 0))
    return pl.pallas_call(
        kern,
        grid_spec=pltpu.PrefetchScalarGridSpec(
            num_scalar_prefetch=1, grid=(nbatch, nblk),
            in_specs=[q_spec, prev, cur, prev, cur],
            out_specs=pl.BlockSpec((blk, qw), lambda b, n, s: (b * nblk + n, 0))),
        out_shape=jax.ShapeDtypeStruct((nbatch * nblk * blk, qw), BF16),
        compiler_params=_cparams("parallel", "parallel"), name="swa_prompt",
    )(sinks, q_src, k_pad, k_pad, v_pad, v_pad)


def _swa_sample_kernel(q_ref, k_ref, v_ref, sink_ref, o_ref, *, scale, kvh, group):
    for g in range(kvh):
        rs = slice(g * group, (g + 1) * group)
        qg = q_ref[rs, :].astype(BF16)
        s = lax.dot_general(qg, k_ref[g].astype(BF16), (((1,), (1,)), ((), ())),
                            preferred_element_type=F32) * scale
        p = _softmax_sink(s, sink_ref[rs, :])
        o_ref[rs, :] = jnp.dot(p.astype(BF16), v_ref[g].astype(BF16),
                               preferred_element_type=F32)


def _swa_sample(q, kc, vc, sinks):
    b, heads, hd = q.shape
    kvh, win = kc.shape[1], kc.shape[2]
    kern = functools.partial(_swa_sample_kernel, scale=hd ** -0.5, kvh=kvh, group=heads // kvh)
    q_spec = pl.BlockSpec((None, heads, hd), lambda i: (i, 0, 0))
    kv_spec = pl.BlockSpec((None, kvh, win, hd), lambda i: (i, 0, 0, 0))
    return pl.pallas_call(
        kern, grid=(b,),
        in_specs=[q_spec, kv_spec, kv_spec, pl.BlockSpec((heads, 1), lambda i: (0, 0))],
        out_specs=q_spec, out_shape=jax.ShapeDtypeStruct((b, heads, hd), F32),
        compiler_params=_cparams("parallel"), name="swa_sample",
    )(q, kc, vc, sinks)


def _softplus(z):
    return jnp.maximum(z, 0.0) + jnp.log1p(jnp.exp(-jnp.abs(z)))


def _rwkv_prep_kernel(*refs, d_lora, i_lora, lora_cols, tiles_per_seq, n_real_tiles):
    groups = [refs[4 * g:4 * g + 4] for g in range(4)]
    mur_ref, muk_ref, muv_ref, mulo_ref, w0_ref, a0_ref, w2_ref, a2_ref, g2_ref = refs[16:25]
    ro_ref, ko_ref, vo_ref, wo_ref, ao_ref, go_ref = refs[25:]
    i = pl.program_id(0)
    is_tail = i == n_real_tiles
    is_start = i % tiles_per_seq == 0

    def shifted(group, mu_ref):
        u_ref, p8_ref, b_ref, t_ref = group
        u = u_ref[...]
        row = lax.broadcasted_iota(I32, u.shape, 0)
        first = jnp.where(is_start, b_ref[0:1, :], p8_ref[SUBLANES - 1:SUBLANES, :])
        prev = jnp.where(row == 0, first, pltpu.roll(u, 1, axis=0))
        prev = jnp.where(is_tail, t_ref[...], prev)
        return u + (prev - u) * mu_ref[...]

    ro_ref[...] = shifted(groups[0], mur_ref)
    ko_ref[...] = shifted(groups[1], muk_ref)
    vo_ref[...] = shifted(groups[2], muv_ref)
    lo = shifted(groups[3], mulo_ref)
    lane = lax.broadcasted_iota(I32, lo.shape, 1)
    lo = jnp.where(lane < lora_cols, lo, 0.0)
    d_hi = _round_up(d_lora, LANES)
    a_hi = _round_up(d_lora + i_lora, LANES)
    g_lo = (d_lora + i_lora) // LANES * LANES
    wl = w0_ref[...] + jnp.dot(jnp.tanh(lo[:, :d_hi]), w2_ref[:d_hi, :], precision=HIGHEST,
                               preferred_element_type=F32)
    w_log = -_softplus(-wl) - 0.5
    wo_ref[...] = jnp.exp(-jnp.exp(w_log))
    al = a0_ref[...] + jnp.dot(lo[:, :a_hi], a2_ref[:a_hi, :], precision=HIGHEST,
                               preferred_element_type=F32)
    ao_ref[...] = jax.nn.sigmoid(al)
    go_ref[...] = jnp.dot(jax.nn.sigmoid(lo[:, g_lo:]), g2_ref[g_lo:, :], precision=HIGHEST,
                          preferred_element_type=F32)


def _rwkv_prep(proj, bnd, tail_prev, mu, w0, a0, w2p, a2p, g2p, *, u_off, width, d_lora, i_lora,
               lora_cols, cw, tr, seq, n_real):
    rows = proj.shape[0]
    assert seq % tr == 0 and n_real % seq == 0 and 0 < rows - n_real <= tr
    nc = width // cw
    ob = u_off // cw
    wb = width // cw
    per8 = tr // SUBLANES

    def group(off, fixed):
        def col(c):
            return off if fixed else off + c
        return [pl.BlockSpec((tr, cw), lambda i, c: (i, ob + col(c))),
                pl.BlockSpec((SUBLANES, cw), lambda i, c: (jnp.maximum(i * per8 - 1, 0), ob + col(c))),
                pl.BlockSpec((SUBLANES, cw), lambda i, c: (0, col(c))),
                pl.BlockSpec((tr, cw), lambda i, c: (0, col(c)))]

    def vec(off, fixed=False):
        return pl.BlockSpec((1, cw), lambda i, c: (0, off if fixed else off + c))

    wspec = pl.BlockSpec((cw, cw), lambda i, c: (0, c))
    out_spec = pl.BlockSpec((tr, cw), lambda i, c: (i, c))
    in_specs = (group(0, False) + group(wb, False) + group(2 * wb, False) + group(3 * wb, True)
                + [vec(0), vec(wb), vec(2 * wb), vec(3 * wb, True), vec(0), vec(0),
                   wspec, wspec, wspec])
    out = jax.ShapeDtypeStruct((rows, width), F32)
    mu2 = mu.reshape(1, -1)
    kern = functools.partial(_rwkv_prep_kernel, d_lora=d_lora, i_lora=i_lora, lora_cols=lora_cols,
                             tiles_per_seq=seq // tr, n_real_tiles=n_real // tr)
    return pl.pallas_call(
        kern, grid=(pl.cdiv(rows, tr), nc), in_specs=in_specs, out_specs=[out_spec] * 6,
        out_shape=[out] * 6, compiler_params=_cparams("parallel", "parallel"), name="rwkv_prep",
    )(*([proj, proj, bnd, tail_prev] * 4), mu2, mu2, mu2, mu2,
      w0.reshape(1, -1), a0.reshape(1, -1), w2p, a2p, g2p)


def _rwkv_scan_kernel(r_ref, w_ref, k_ref, v_ref, a_ref, kk_ref, ka_ref, rk_ref, gw_ref, gb_ref,
                      s0_ref, y_ref, sfin_ref, state, a_s, b_s, k_s, wr_s, *, tc, nt, hd):
    tci = pl.program_id(1)

    @pl.when(tci == 0)
    def _():
        state[...] = s0_ref[...]

    sub = lax.broadcasted_iota(I32, (SUBLANES, LANES), 0)

    def step(t, carry):
        r = r_ref[t]
        w = w_ref[t]
        k = k_ref[t]
        v = v_ref[t]
        a = a_ref[t]
        kk = k * kk_ref[...]
        nrm = jnp.sqrt(jnp.sum(kk * kk, axis=0, keepdims=True))
        kk = kk / jnp.maximum(nrm, 1e-12)
        k2 = k * (1.0 + (a - 1.0) * ka_ref[...])
        bt = kk * a
        a_s[...] = -kk
        b_s[...] = bt
        k_s[...] = k2
        wr_s[...] = w * r
        rb = jnp.sum(bt * r, axis=0, keepdims=True)
        rk = jnp.sum(k2 * r, axis=0, keepdims=True)
        bonus = jnp.sum(r * k2 * rk_ref[...], axis=0, keepdims=True)

        def vgroup(gi, c2):
            base = pl.multiple_of(gi * SUBLANES, SUBLANES)
            vt = v_ref[t, pl.ds(base, SUBLANES), :]
            yt = jnp.zeros((SUBLANES, LANES), F32)
            for j in range(SUBLANES):
                sv = state[base + j]
                sa = jnp.sum(sv * a_s[...], axis=0, keepdims=True)
                y0 = jnp.sum(sv * wr_s[...], axis=0, keepdims=True)
                vv = vt[j:j + 1, :]
                state[base + j] = sv * w_ref[t] + sa * b_s[...] + vv * k_s[...]
                yt = jnp.where(sub == j, y0 + sa * rb + vv * rk, yt)
            y_ref[t, pl.ds(base, SUBLANES), :] = yt
            return c2

        lax.fori_loop(0, hd // SUBLANES, vgroup, 0)
        y = y_ref[t]
        mu = jnp.mean(y, axis=0, keepdims=True)
        yc = y - mu
        var = jnp.mean(yc * yc, axis=0, keepdims=True)
        y_ref[t] = yc * lax.rsqrt(var + GN_EPS) * gw_ref[...] + gb_ref[...] + bonus * v
        return carry

    lax.fori_loop(0, tc, step, 0)

    @pl.when(tci == nt - 1)
    def _():
        sfin_ref[...] = state[...]


def _rwkv_scan(r, w, k, v, a, params, s0):
    t, hd, lanes = r.shape
    tc = _div_tile(t, 48, 1)
    nt = t // tc
    step_spec = pl.BlockSpec((tc, hd, LANES), lambda g, c: (c, 0, g))
    par_spec = pl.BlockSpec((hd, LANES), lambda g, c: (0, g))
    st_spec = pl.BlockSpec((hd, hd, LANES), lambda g, c: (0, 0, g))
    tile = pltpu.VMEM((hd, LANES), F32)
    return pl.pallas_call(
        functools.partial(_rwkv_scan_kernel, tc=tc, nt=nt, hd=hd),
        grid=(lanes // LANES, nt),
        in_specs=[step_spec] * 5 + [par_spec] * 5 + [st_spec],
        out_specs=[step_spec, st_spec],
        out_shape=[jax.ShapeDtypeStruct((t, hd, lanes), F32),
                   jax.ShapeDtypeStruct((hd, hd, lanes), F32)],
        scratch_shapes=[pltpu.VMEM((hd, hd, LANES), F32), tile, tile, tile, tile],
        compiler_params=_cparams("parallel", "arbitrary"), name="rwkv_scan",
    )(r, w, k, v, a, *params, s0)


def _mul_kernel(a_ref, b_ref, o_ref):
    o_ref[...] = (a_ref[...] * b_ref[...]).astype(o_ref.dtype)


def _mul_bf16(a, b):
    r, d = a.shape
    tr = _div_tile(r, 512, SUBLANES)
    spec = pl.BlockSpec((tr, d), lambda i: (i, 0))
    return pl.pallas_call(_mul_kernel, grid=(r // tr,), in_specs=[spec, spec], out_specs=spec,
                          out_shape=jax.ShapeDtypeStruct((r, d), BF16),
                          compiler_params=_cparams("parallel"), name="gate_mul")(a, b)


def _router_kernel(x_ref, w_ref, idx_ref, gate_ref, *, n_exp):
    logits = jnp.dot(x_ref[...], w_ref[...], precision=HIGHEST, preferred_element_type=F32)
    lane = lax.broadcasted_iota(I32, logits.shape, 1)
    lg = jnp.where(lane < n_exp, logits, -jnp.inf)
    m1 = jnp.max(lg, axis=-1, keepdims=True)
    i1 = jnp.min(jnp.where(lg == m1, lane, LANES), axis=-1, keepdims=True)
    lg2 = jnp.where(lane == i1, -jnp.inf, lg)
    m2 = jnp.max(lg2, axis=-1, keepdims=True)
    i2 = jnp.min(jnp.where(lg2 == m2, lane, LANES), axis=-1, keepdims=True)
    e = jnp.exp(m2 - m1)
    g1 = 1.0 / (1.0 + e)
    g2 = e / (1.0 + e)
    idx_ref[...] = jnp.where(lane == 0, i1, jnp.where(lane == 1, i2, 0))
    gate_ref[...] = jnp.where(lane == 0, g1, jnp.where(lane == 1, g2, 0.0))


def _router(x, w_router):
    r, d = x.shape
    n_exp = w_router.shape[1]
    wp = jnp.zeros((d, LANES), F32).at[:, :n_exp].set(w_router)
    tr = _div_tile(r, 512, SUBLANES)
    row = pl.BlockSpec((tr, d), lambda i: (i, 0))
    out = pl.BlockSpec((tr, LANES), lambda i: (i, 0))
    return pl.pallas_call(
        functools.partial(_router_kernel, n_exp=n_exp), grid=(r // tr,),
        in_specs=[row, pl.BlockSpec((d, LANES), lambda i: (0, 0))], out_specs=(out, out),
        out_shape=(jax.ShapeDtypeStruct((r, LANES), I32), jax.ShapeDtypeStruct((r, LANES), F32)),
        compiler_params=_cparams("parallel"), name="router",
    )(x, wp)


def _row_copy(src_hbm, row, dst, r, sem):
    return pltpu.make_async_copy(src_hbm.at[pl.ds(row, 1), :], dst.at[pl.ds(r, 1), :], sem)


def _dispatch_kernel(src_ref, nused_ref, x_hbm, o_ref, buf, sem, *, tg):
    base = pl.program_id(0) * tg

    @pl.when(base < nused_ref[0])
    def _():
        def issue(r, c):
            _row_copy(x_hbm, src_ref[base + r], buf, r, sem).start()
            return c

        def wait(r, c):
            _row_copy(x_hbm, 0, buf, r, sem).wait()
            return c

        lax.fori_loop(0, tg, issue, 0)
        lax.fori_loop(0, tg, wait, 0)
        o_ref[...] = buf[...].astype(BF16)

    @pl.when(base >= nused_ref[0])
    def _():
        o_ref[...] = jnp.zeros_like(o_ref)


def _dispatch(x, src, nused, *, tg):
    p = src.shape[0]
    d = x.shape[1]
    return pl.pallas_call(
        functools.partial(_dispatch_kernel, tg=tg),
        grid_spec=pltpu.PrefetchScalarGridSpec(
            num_scalar_prefetch=2, grid=(p // tg,),
            in_specs=[pl.BlockSpec(memory_space=pl.ANY)],
            out_specs=pl.BlockSpec((tg, d), lambda g, s, n: (g, 0)),
            scratch_shapes=[pltpu.VMEM((tg, d), F32), pltpu.SemaphoreType.DMA(())]),
        out_shape=jax.ShapeDtypeStruct((p, d), BF16),
        compiler_params=_cparams("arbitrary"), name="moe_dispatch",
    )(src, nused, x)


def _combine_kernel(d1_ref, d2_ref, o_hbm, gate_ref, x_ref, g_ref, b_ref, xo_ref, xb_ref,
                    buf1, buf2, sem1, sem2, *, tg, alpha):
    base = pl.program_id(0) * tg

    def issue(r, c):
        _row_copy(o_hbm, d1_ref[base + r], buf1, r, sem1).start()
        _row_copy(o_hbm, d2_ref[base + r], buf2, r, sem2).start()
        return c

    def wait(r, c):
        _row_copy(o_hbm, 0, buf1, r, sem1).wait()
        _row_copy(o_hbm, 0, buf2, r, sem2).wait()
        return c

    lax.fori_loop(0, tg, issue, 0)
    lax.fori_loop(0, tg, wait, 0)
    gates = gate_ref[...]
    y = gates[:, 0:1] * buf1[...] + gates[:, 1:2] * buf2[...]
    out = _ln_rows(alpha * x_ref[...] + y, g_ref[...], b_ref[...])
    xo_ref[...] = out
    xb_ref[...] = out.astype(BF16)


def _combine_ln(o_sorted, d1, d2, gates, x, g, b, *, alpha, tg):
    r, d = x.shape
    row = pl.BlockSpec((tg, d), lambda i, a, c: (i, 0))
    vec = pl.BlockSpec((1, d), lambda i, a, c: (0, 0))
    return pl.pallas_call(
        functools.partial(_combine_kernel, tg=tg, alpha=alpha),
        grid_spec=pltpu.PrefetchScalarGridSpec(
            num_scalar_prefetch=2, grid=(r // tg,),
            in_specs=[pl.BlockSpec(memory_space=pl.ANY),
                      pl.BlockSpec((tg, LANES), lambda i, a, c: (i, 0)), row, vec, vec],
            out_specs=(row, row),
            scratch_shapes=[pltpu.VMEM((tg, d), F32), pltpu.VMEM((tg, d), F32),
                            pltpu.SemaphoreType.DMA(()), pltpu.SemaphoreType.DMA(())]),
        out_shape=(jax.ShapeDtypeStruct((r, d), F32), jax.ShapeDtypeStruct((r, d), BF16)),
        compiler_params=_cparams("arbitrary"), name="moe_combine_ln",
    )(d1, d2, o_sorted, gates, x, g.reshape(1, d), b.reshape(1, d))


def _route_plan(idx, n_exp, tm):
    r = idx.shape[0]
    flat = idx.reshape(-1)
    onehot = (flat[:, None] == jnp.arange(n_exp, dtype=I32)[None, :]).astype(I32)
    rank = jnp.sum((jnp.cumsum(onehot, axis=0) - onehot) * onehot, axis=1)
    sizes = jnp.sum(onehot, axis=0)
    etiles = (sizes + tm - 1) // tm
    tend = jnp.cumsum(etiles)
    tstart = tend - etiles
    dest = tstart[flat] * tm + rank
    nt = pl.cdiv(2 * r, tm) + n_exp
    tid = jnp.arange(nt, dtype=I32)
    n_used = tend[-1]
    te = jnp.minimum(jnp.sum((tid[:, None] >= tend[None, :]).astype(I32), axis=1), n_exp - 1)
    used = tid < n_used
    last = jnp.maximum(n_used - 1, 0)
    tidx = jnp.minimum(tid, last)
    texp = te[tidx]
    valid = jnp.where(used, jnp.clip(sizes[texp] - (tid - tstart[texp]) * tm, 0, tm), 0)
    prev = jnp.concatenate([jnp.full((1,), -1, I32), texp[:-1]])
    wnew = jnp.logical_and(used, texp != prev).astype(I32)
    src = jnp.zeros((nt * tm,), I32).at[dest].set(jnp.arange(2 * r, dtype=I32) // 2)
    sched = (valid.astype(I32), tidx.astype(I32), texp.astype(I32), wnew)
    dest2 = dest.reshape(r, 2).astype(I32)
    return sched, src, (n_used * tm).reshape(1).astype(I32), dest2[:, 0], dest2[:, 1]


def kernel(x_prompt, x_sample, cache_swa_k, cache_swa_v, state_rwkv, state_shift, meta_tokens, ln_emb_g, ln_emb_b, w_in, att_sinks, rwkv_mu, rwkv_w0, rwkv_w2, rwkv_a0, rwkv_a2, rwkv_g2, rwkv_k_k, rwkv_k_a, rwkv_r_k, rwkv_gn_w, rwkv_gn_b, w_proj_att, w_proj_rwkv, w_out, ln_mix_g, ln_mix_b, ln_ffn_g, ln_ffn_b, ffn_w_gate, ffn_w_up, ffn_w_down, moe_router, moe_w_gate, moe_w_up, moe_w_down):
    nb, seq, dm = x_prompt.shape
    db, dseq, _ = x_sample.shape
    assert dseq == 1
    depth = w_in.shape[0]
    n_meta = meta_tokens.shape[0]
    _, _, window, kvh, hd = cache_swa_k.shape
    heads = att_sinks.shape[1]
    rh = state_rwkv.shape[2]
    att_w = heads * hd
    kv_w = kvh * hd
    rw = rh * hd
    shift_cols = state_shift.shape[2]
    lora_cols = shift_cols - 3 * rw
    d_lora, i_lora, g_lora = rwkv_w2.shape[1], rwkv_a2.shape[1], rwkv_g2.shape[1]
    assert d_lora + i_lora + g_lora == lora_cols
    n_exp = moe_router.shape[2]
    blk = window
    assert seq % blk == 0 and n_meta <= blk and (nb * rh) % LANES == 0 and (db * rh) % LANES == 0
    nblk = seq // blk
    lead = blk - n_meta
    alpha = (2 * depth) ** 0.25

    nr = nb * seq
    m0, s0r = nr, nr + n_meta
    rows = nr + _round_up(n_meta + db, 64)
    u_off = att_w + 2 * kv_w
    g_off = u_off + shift_cols
    cw = 512
    assert u_off % cw == 0 and rw % cw == 0 and lora_cols <= cw

    tm = 1024
    tr = 256
    tk_down = 1024
    sched = _dense_sched(rows, tm)

    def tail_rows(meta_part, sample_part):
        cols = meta_part.shape[1]
        return jnp.concatenate([meta_part, sample_part,
                                jnp.zeros((tr - n_meta - db, cols), F32)], axis=0)

    x, xb = _embed_ln(x_prompt.reshape(nr, dm), tail_rows(meta_tokens, x_sample.reshape(db, dm)),
                      ln_emb_g, ln_emb_b, rows, tr)

    def to_lanes_prompt(z):
        real = z[:nr].reshape(nb, seq, rh, hd)
        meta = jnp.broadcast_to(z[m0:m0 + n_meta].reshape(1, n_meta, rh, hd), (nb, n_meta, rh, hd))
        full = jnp.concatenate([meta, real], axis=1)
        return full.transpose(1, 3, 0, 2).reshape(n_meta + seq, hd, nb * rh)

    def to_lanes_sample(z):
        return z[s0r:s0r + db].reshape(db, rh, hd).transpose(2, 0, 1).reshape(1, hd, db * rh)

    def head_param(p, nbatch):
        return jnp.broadcast_to(p.reshape(rh, hd).T[:, None, :], (hd, nbatch, rh)).reshape(hd, nbatch * rh)

    outs = {k: [] for k in ("pk", "pv", "ps", "psh", "sk", "sv", "ss", "ssh")}
    for l in range(depth):
        proj = _matmul_ws([xb], [(w_in, (l,))], [(0, 0)], [], _ep_first, F32, g_off,
                          tm=tm, tn=512, sched=sched, name="proj_in")
        gates = _matmul_ws([xb], [(w_in[l, :, g_off:], ())], [(0, 0)], [], _ep_first, F32,
                           2 * dm, tm=tm, tn=512, sched=sched, name="proj_gates")

        k_all = proj[:, att_w:att_w + kv_w]
        v_all = proj[:, att_w + kv_w:att_w + 2 * kv_w]
        sinks = att_sinks[l]

        def padded(z):
            lead_blk = jnp.concatenate([jnp.zeros((lead, kv_w), F32), z[m0:m0 + n_meta]], axis=0)
            real = z[:nr].reshape(nb, seq, kv_w)
            full = jnp.concatenate([jnp.broadcast_to(lead_blk[None], (nb, blk, kv_w)), real], axis=1)
            meta_only = jnp.concatenate([jnp.zeros((blk, kv_w), F32), lead_blk], axis=0)[None]
            return full, meta_only

        k_pad, k_meta = padded(k_all)
        v_pad, v_meta = padded(v_all)
        swa = functools.partial(_swa_prompt, lead=lead, heads=heads, kvh=kvh, hd=hd, blk=blk)
        o_real = swa(proj, k_pad, v_pad, sinks, nbatch=nb, nblk=nblk, n_off=0)
        q_meta = jnp.concatenate([jnp.zeros((lead, att_w), F32), proj[m0:m0 + n_meta, :att_w]], axis=0)
        o_meta = swa(q_meta, k_meta, v_meta, sinks, nbatch=1, nblk=1, n_off=-1)[lead:]

        k_new = k_all[s0r:s0r + db].reshape(db, 1, kvh, hd)
        v_new = v_all[s0r:s0r + db].reshape(db, 1, kvh, hd)
        kc = jnp.concatenate([cache_swa_k[l][:, 1:], k_new], axis=1)
        vc = jnp.concatenate([cache_swa_v[l][:, 1:], v_new], axis=1)
        q_s = proj[s0r:s0r + db, :att_w].reshape(db, heads, hd)
        o_s = _swa_sample(q_s, kc.transpose(0, 2, 1, 3), vc.transpose(0, 2, 1, 3),
                          sinks.reshape(heads, 1))
        y_att = jnp.concatenate([o_real, o_meta, o_s.reshape(db, att_w).astype(BF16),
                                 jnp.zeros((rows - s0r - db, att_w), BF16)], axis=0)
        outs["pk"].append(k_all[:nr].reshape(nb, seq, kvh, hd)[:, seq - window:])
        outs["pv"].append(v_all[:nr].reshape(nb, seq, kvh, hd)[:, seq - window:])
        outs["sk"].append(kc)
        outs["sv"].append(vc)

        u_meta = proj[m0:m0 + n_meta, u_off:u_off + shift_cols]
        u_sample = proj[s0r:s0r + db, u_off:u_off + shift_cols]
        u_last = proj[seq - 1:nr:seq, u_off:u_off + shift_cols]
        bnd = jnp.concatenate([u_meta[n_meta - 1:], jnp.zeros((SUBLANES - 1, shift_cols), F32)], axis=0)
        tail_prev = tail_rows(
            jnp.concatenate([jnp.zeros((1, shift_cols), F32), u_meta[:-1]], axis=0), state_shift[l])
        w2p = jnp.zeros((cw, rw), F32).at[:d_lora].set(rwkv_w2[l])
        a2p = jnp.zeros((cw, rw), F32).at[d_lora:d_lora + i_lora].set(rwkv_a2[l])
        g2p = jnp.zeros((cw, rw), F32).at[d_lora + i_lora:lora_cols].set(rwkv_g2[l])
        r_n, k_n, v_n, w_n, a_n, g_n = _rwkv_prep(
            proj, bnd, tail_prev, rwkv_mu[l], rwkv_w0[l], rwkv_a0[l], w2p, a2p, g2p,
            u_off=u_off, width=rw, d_lora=d_lora, i_lora=i_lora, lora_cols=lora_cols, cw=cw,
            tr=tr, seq=seq, n_real=nr)

        hp = (rwkv_k_k[l], rwkv_k_a[l], rwkv_r_k[l], rwkv_gn_w[l], rwkv_gn_b[l])
        y_p, s_p = _rwkv_scan(*[to_lanes_prompt(z) for z in (r_n, w_n, k_n, v_n, a_n)],
                              [head_param(p, nb) for p in hp],
                              jnp.zeros((hd, hd, nb * rh), F32))
        y_s, s_s = _rwkv_scan(*[to_lanes_sample(z) for z in (r_n, w_n, k_n, v_n, a_n)],
                              [head_param(p, db) for p in hp],
                              state_rwkv[l].transpose(2, 3, 0, 1).reshape(hd, hd, db * rh))
        y_p = y_p.reshape(n_meta + seq, hd, nb, rh).transpose(2, 0, 3, 1)
        y_rwkv = jnp.concatenate([
            y_p[:, n_meta:].reshape(nr, rw), y_p[0, :n_meta].reshape(n_meta, rw),
            y_s.reshape(hd, db, rh).transpose(1, 2, 0).reshape(db, rw),
            jnp.zeros((rows - s0r - db, rw), F32)], axis=0)
        yg = _mul_bf16(y_rwkv, g_n)
        outs["ps"].append(s_p.reshape(hd, hd, nb, rh).transpose(2, 3, 0, 1))
        outs["ss"].append(s_s.reshape(hd, hd, db, rh).transpose(2, 3, 0, 1))
        outs["psh"].append(u_last)
        outs["ssh"].append(u_sample)

        tn_merge = min(512, dm)
        merged = _matmul_ws([y_att, yg], [(w_proj_att, (l,)), (w_proj_rwkv, (l,))],
                            [(0, 0), (1, 1)], [(gates, 0), (gates, dm // tn_merge)], _ep_merge,
                            BF16, dm, tm=tm, tn=tn_merge, sched=sched, name="merge")
        mix = _matmul_ws([merged], [(w_out, (l,))], [(0, 0)], [], _ep_first, F32, dm,
                         tm=tm, tn=512, sched=sched, name="proj_out")
        x, xb = _res_layer_norm(x, mix, ln_mix_g[l], ln_mix_b[l], alpha)

        i = l // 2
        if l % 2 == 0:
            h = _matmul_ws([xb], [(ffn_w_gate, (i,)), (ffn_w_up, (i,))], [(0, 0), (0, 1)], [],
                           _ep_swiglu, BF16, _round_up(ffn_w_gate.shape[2], tk_down), tm=tm,
                           tn=256, sched=sched, name="ffn_gate_up")
            y = _matmul_kacc(h, ffn_w_down, (i,), tm=tm, tn=2048, tk=tk_down, sched=sched,
                             name="ffn_down")
            x, xb = _res_layer_norm(x, y, ln_ffn_g[l], ln_ffn_b[l], alpha)
        else:
            idx, gates = _router(x, moe_router[i])
            msched, src, nused, d1, d2 = _route_plan(idx[:, :2], n_exp, tm)
            xs = _dispatch(x, src, nused, tg=256)
            h = _matmul_ws([xs], [(moe_w_gate, (i, "e")), (moe_w_up, (i, "e"))],
                           [(0, 0), (0, 1)], [], _ep_swiglu, BF16,
                           _round_up(moe_w_gate.shape[3], tk_down), tm=tm, tn=256, sched=msched,
                           name="moe_gate_up")
            o = _matmul_kacc(h, moe_w_down, (i, "e"), tm=tm, tn=2048, tk=tk_down, sched=msched,
                             name="moe_down")
            x, xb = _combine_ln(o, d1, d2, gates, x, ln_ffn_g[l], ln_ffn_b[l], alpha=alpha,
                                tg=_div_tile(rows, 128, SUBLANES))

    y_prompt = x[:nr].reshape(nb, seq, dm)
    y_sample = x[s0r:s0r + db].reshape(db, 1, dm)
    st = {k: jnp.stack(v) for k, v in outs.items()}
    return (y_prompt, y_sample, st["pk"], st["pv"], st["ps"], st["psh"],
            st["sk"], st["sv"], st["ss"], st["ssh"])
```

```python
import functools

import jax
import jax.numpy as jnp
from jax import lax
from jax.experimental import pallas as pl
from jax.experimental.pallas import tpu as pltpu

F32 = jnp.float32
BF16 = jnp.bfloat16
I32 = jnp.int32

LANES = 128
SUBLANES = 8
VMEM_LIMIT = 56 * 1024 * 1024
LN_EPS = 1e-5
GN_EPS = 64e-5
NEG_BIG = -1e30
HIGHEST = lax.Precision.HIGHEST


def _cparams(*sem):
    return pltpu.CompilerParams(dimension_semantics=sem, vmem_limit_bytes=VMEM_LIMIT)


def _round_up(x, m):
    return (x + m - 1) // m * m


def _div_tile(n, target, mult):
    best = None
    for t in range(mult, min(n, target) + 1, mult):
        if n % t == 0:
            best = t
    assert best is not None, (n, target, mult)
    return best


def _ln_rows(x, g, b):
    mu = jnp.mean(x, axis=-1, keepdims=True)
    xc = x - mu
    var = jnp.mean(xc * xc, axis=-1, keepdims=True)
    return xc * lax.rsqrt(var + LN_EPS) * g + b


def _embed_ln_kernel(x_ref, t_ref, g_ref, b_ref, o_ref, ob_ref, *, n_real_tiles):
    x = jnp.where(pl.program_id(0) == n_real_tiles, t_ref[...], x_ref[...])
    y = _ln_rows(x, g_ref[...], b_ref[...])
    o_ref[...] = y
    ob_ref[...] = y.astype(BF16)


def _embed_ln(x_real, x_tail, g, b, rows, tr):
    n_real, d = x_real.shape
    assert n_real % tr == 0 and x_tail.shape == (tr, d) and 0 < rows - n_real <= tr
    nrt = n_real // tr
    vec = pl.BlockSpec((1, d), lambda i: (0, 0))
    row = pl.BlockSpec((tr, d), lambda i: (i, 0))
    return pl.pallas_call(
        functools.partial(_embed_ln_kernel, n_real_tiles=nrt), grid=(nrt + 1,),
        in_specs=[pl.BlockSpec((tr, d), lambda i: (jnp.minimum(i, nrt - 1), 0)),
                  pl.BlockSpec((tr, d), lambda i: (0, 0)), vec, vec],
        out_specs=(row, row),
        out_shape=(jax.ShapeDtypeStruct((rows, d), F32), jax.ShapeDtypeStruct((rows, d), BF16)),
        compiler_params=_cparams("arbitrary"), name="embed_ln",
    )(x_real, x_tail, g.reshape(1, d), b.reshape(1, d))


def _res_ln_kernel(x_ref, y_ref, g_ref, b_ref, o_ref, ob_ref, *, alpha):
    y = _ln_rows(alpha * x_ref[...] + y_ref[...], g_ref[...], b_ref[...])
    o_ref[...] = y
    ob_ref[...] = y.astype(BF16)


def _res_layer_norm(x, y, g, b, alpha):
    r, d = x.shape
    tr = _div_tile(r, 256, SUBLANES)
    row = pl.BlockSpec((tr, d), lambda i: (i, 0))
    vec = pl.BlockSpec((1, d), lambda i: (0, 0))
    out_shape = (jax.ShapeDtypeStruct((r, d), F32), jax.ShapeDtypeStruct((r, d), BF16))
    return pl.pallas_call(
        functools.partial(_res_ln_kernel, alpha=alpha), grid=(r // tr,),
        in_specs=[row, row, vec, vec], out_specs=(row, row), out_shape=out_shape,
        compiler_params=_cparams("parallel"), name="res_ln",
    )(x, y, g.reshape(1, d), b.reshape(1, d))


SUB_ROWS = 256
KACC_COLS = 1024


def _tile_rows(valid, tm, fn, zero_fn):
    @pl.when(valid == tm)
    def _():
        fn(slice(None))

    @pl.when(valid < tm)
    def _():
        for s in range(tm // SUB_ROWS):
            rows = pl.ds(s * SUB_ROWS, SUB_ROWS)

            @pl.when(s * SUB_ROWS < valid)
            def _():
                fn(rows)

            @pl.when(s * SUB_ROWS >= valid)
            def _():
                zero_fn(rows)


def _ws_kernel(valid_ref, tidx_ref, texp_ref, wnew_ref, *refs, n_lhs, n_w, n_extra, pairs,
               epilogue, tm, tn, n_real):
    lhs = refs[:n_lhs]
    ws = refs[n_lhs:n_lhs + n_w]
    extras = refs[n_lhs + n_w:n_lhs + n_w + n_extra]
    o_ref = refs[n_lhs + n_w + n_extra]
    wb = refs[n_lhs + n_w + n_extra + 1:]
    j = pl.program_id(0)
    i = pl.program_id(1)

    @pl.when(jnp.logical_or(i == 0, wnew_ref[i] == 1))
    def _():
        for w_ref, wb_ref in zip(ws, wb):
            wb_ref[...] = w_ref[...].astype(BF16)

    def compute(rows):
        accs = [jnp.dot(lhs[li][rows, :], wb[wi][...], preferred_element_type=F32)
                for li, wi in pairs]
        ex = [e[rows, :] for e in extras]
        res = epilogue(accs, ex)
        if n_real is not None:
            col = lax.broadcasted_iota(I32, res.shape, 1)
            res = jnp.where(col < n_real - j * tn, res, 0.0)
        o_ref[rows, :] = res.astype(o_ref.dtype)

    def zero(rows):
        o_ref[rows, :] = jnp.zeros((SUB_ROWS, o_ref.shape[1]), o_ref.dtype)

    _tile_rows(valid_ref[i], tm, compute, zero)


def _matmul_ws(lhs, ws, pairs, extras, epilogue, out_dtype, n_out, *, tm, tn, sched, name):
    valid, tidx, texp, wnew = sched
    rows, _ = lhs[0].shape
    nm = valid.shape[0]
    tn = min(tn, n_out)
    nn = pl.cdiv(n_out, tn)
    w_cols = ws[0][0].shape[-1]
    last_w_blk = pl.cdiv(w_cols, tn) - 1
    n_real = w_cols if n_out > w_cols else None

    def lhs_map(j, i, v, t, e, wn):
        return (t[i], 0)

    def w_map(lead):
        def m(j, i, v, t, e, wn):
            return tuple(e[i] if s == "e" else s for s in lead) + (0, jnp.minimum(j, last_w_blk))
        return m

    def out_map(j, i, v, t, e, wn):
        return (i, j)

    def extra_map(off):
        return lambda j, i, v, t, e, wn: (i, j + off)

    in_specs = [pl.BlockSpec((tm, x.shape[1]), lhs_map) for x in lhs]
    w_scratch = []
    for arr, lead in ws:
        k = arr.shape[-2]
        in_specs.append(pl.BlockSpec((None,) * len(lead) + (k, tn), w_map(lead)))
        w_scratch.append(pltpu.VMEM((k, tn), BF16))
    in_specs += [pl.BlockSpec((tm, tn), extra_map(off)) for _, off in extras]
    kern = functools.partial(_ws_kernel, n_lhs=len(lhs), n_w=len(ws), n_extra=len(extras),
                             pairs=pairs, epilogue=epilogue, tm=tm, tn=tn, n_real=n_real)
    return pl.pallas_call(
        kern,
        grid_spec=pltpu.PrefetchScalarGridSpec(
            num_scalar_prefetch=4, grid=(nn, nm), in_specs=in_specs,
            out_specs=pl.BlockSpec((tm, tn), out_map), scratch_shapes=w_scratch),
        out_shape=jax.ShapeDtypeStruct((rows, n_out), out_dtype),
        compiler_params=_cparams("parallel", "arbitrary"), name=name,
    )(valid, tidx, texp, wnew, *lhs, *[a for a, _ in ws], *[a for a, _ in extras])


def _kacc_kernel(valid_ref, tidx_ref, texp_ref, x_ref, w_ref, o_ref, *, tm, tk, k_real, k_pad):
    i = pl.program_id(0)
    k = pl.program_id(2)
    n = o_ref.shape[1]
    cn = min(n, KACC_COLS)

    def compute(rows):
        for c in range(n // cn):
            cols = slice(c * cn, (c + 1) * cn)
            w = w_ref[:, cols]
            if k_pad != k_real:
                row = lax.broadcasted_iota(I32, w.shape, 0)
                w = jnp.where(row < k_real - k * tk, w, 0.0)
            part = jnp.dot(x_ref[rows, :], w.astype(BF16), preferred_element_type=F32)

            @pl.when(k == 0)
            def _():
                o_ref[rows, cols] = part

            @pl.when(k > 0)
            def _():
                o_ref[rows, cols] += part

    def zero(rows):
        @pl.when(k == 0)
        def _():
            o_ref[rows, :] = jnp.zeros((SUB_ROWS, n), o_ref.dtype)

    _tile_rows(valid_ref[i], tm, compute, zero)


def _matmul_kacc(x, w, lead, *, tm, tn, tk, sched, name):
    valid, tidx, texp, _ = sched
    rows, k_pad = x.shape
    k_real, n = w.shape[-2:]
    nm = valid.shape[0]
    tn = min(tn, n)
    tk = min(tk, k_pad)
    assert k_pad % tk == 0 and n % tn == 0 and k_pad - k_real < tk
    nk = k_pad // tk

    def x_map(i, j, k, v, t, e):
        return (t[i], jnp.where(v[i] > 0, k, nk - 1))

    def w_map(i, j, k, v, t, e):
        return tuple(e[i] if s == "e" else s for s in lead) + (jnp.where(v[i] > 0, k, nk - 1), j)

    def o_map(i, j, k, v, t, e):
        return (i, j)

    return pl.pallas_call(
        functools.partial(_kacc_kernel, tm=tm, tk=tk, k_real=k_real, k_pad=k_pad),
        grid_spec=pltpu.PrefetchScalarGridSpec(
            num_scalar_prefetch=3, grid=(nm, n // tn, nk),
            in_specs=[pl.BlockSpec((tm, tk), x_map),
                      pl.BlockSpec((None,) * len(lead) + (tk, tn), w_map)],
            out_specs=pl.BlockSpec((tm, tn), o_map)),
        out_shape=jax.ShapeDtypeStruct((rows, n), F32),
        compiler_params=_cparams("arbitrary", "arbitrary", "arbitrary"), name=name,
    )(valid, tidx, texp, x, w)


def _dense_sched(rows, tm):
    nm = pl.cdiv(rows, tm)
    valid = [tm] * nm
    valid[-1] = rows - (nm - 1) * tm
    z = jnp.zeros((nm,), I32)
    return (jnp.asarray(valid, I32), jnp.arange(nm, dtype=I32), z, z)


def _ep_first(accs, ex):
    return accs[0]


def _ep_swiglu(accs, ex):
    g, u = accs
    return g * jax.nn.sigmoid(g) * u


def _ep_merge(accs, ex):
    return jax.nn.sigmoid(ex[0]) * accs[0] + jax.nn.sigmoid(ex[1]) * accs[1]


def _softmax_sink(s, sink):
    m = jnp.maximum(jnp.max(s, axis=-1, keepdims=True), sink)
    p = jnp.exp(s - m)
    den = jnp.sum(p, axis=-1, keepdims=True) + jnp.exp(sink - m)
    return p / den


def _swa_kernel(sink_ref, q_ref, kp_ref, kc_ref, vp_ref, vc_ref, o_ref, *, n_off, lead, scale,
                kvh, group, hd, blk):
    n = pl.program_id(1)
    qi = lax.broadcasted_iota(I32, (blk, 2 * blk), 0)
    kj2 = lax.broadcasted_iota(I32, (blk, 2 * blk), 1)
    coord = (n + n_off) * blk + kj2
    window = jnp.logical_and(kj2 > qi, kj2 <= qi + blk)
    mask = jnp.logical_and(coord >= lead, window)
    outs = []
    for g in range(kvh):
        cs = slice(g * hd, (g + 1) * hd)
        kb = jnp.concatenate([kp_ref[:, cs], kc_ref[:, cs]], axis=0).astype(BF16)
        vb = jnp.concatenate([vp_ref[:, cs], vc_ref[:, cs]], axis=0).astype(BF16)
        for h in range(group):
            hh = g * group + h
            qh = q_ref[:, hh * hd:(hh + 1) * hd].astype(BF16)
            s = lax.dot_general(qh, kb, (((1,), (1,)), ((), ())),
                                preferred_element_type=F32) * scale
            s = jnp.where(mask, s, NEG_BIG)
            p = _softmax_sink(s, sink_ref[hh])
            outs.append(jnp.dot(p.astype(BF16), vb, preferred_element_type=F32))
    o_ref[...] = jnp.concatenate(outs, axis=-1).astype(o_ref.dtype)


def _swa_prompt(q_src, k_pad, v_pad, sinks, *, nbatch, nblk, n_off, lead, heads, kvh, hd, blk):
    qw = heads * hd
    kw = kvh * hd
    kern = functools.partial(_swa_kernel, n_off=n_off, lead=lead, scale=hd ** -0.5, kvh=kvh,
                             group=heads // kvh, hd=hd, blk=blk)
    q_spec = pl.BlockSpec((blk, qw), lambda b, n, s: (b * nblk + n, 0))
    prev = pl.BlockSpec((None, blk, kw), lambda b, n, s: (b, n, 0))
    cur = pl.BlockSpec((None, blk, kw), lambda b, n, s: (b, n + 1, 0))
    return pl.pallas_call(
        kern,
        grid_spec=pltpu.PrefetchScalarGridSpec(
            num_scalar_prefetch=1, grid=(nbatch, nblk),
            in_specs=[q_spec, prev, cur, prev, cur],
            out_specs=pl.BlockSpec((blk, qw), lambda b, n, s: (b * nblk + n, 0))),
        out_shape=jax.ShapeDtypeStruct((nbatch * nblk * blk, qw), BF16),
        compiler_params=_cparams("parallel", "parallel"), name="swa_prompt",
    )(sinks, q_src, k_pad, k_pad, v_pad, v_pad)


def _swa_sample_kernel(q_ref, k_ref, v_ref, sink_ref, o_ref, *, scale, kvh, group):
    for g in range(kvh):
        rs = slice(g * group, (g + 1) * group)
        qg = q_ref[rs, :].astype(BF16)
        s = lax.dot_general(qg, k_ref[g].astype(BF16), (((1,), (1,)), ((), ())),
                            preferred_element_type=F32) * scale
        p = _softmax_sink(s, sink_ref[rs, :])
        o_ref[rs, :] = jnp.dot(p.astype(BF16), v_ref[g].astype(BF16),
                               preferred_element_type=F32)


def _swa_sample(q, kc, vc, sinks):
    b, heads, hd = q.shape
    kvh, win = kc.shape[1], kc.shape[2]
    kern = functools.partial(_swa_sample_kernel, scale=hd ** -0.5, kvh=kvh, group=heads // kvh)
    q_spec = pl.BlockSpec((None, heads, hd), lambda i: (i, 0, 0))
    kv_spec = pl.BlockSpec((None, kvh, win, hd), lambda i: (i, 0, 0, 0))
    return pl.pallas_call(
        kern, grid=(b,),
        in_specs=[q_spec, kv_spec, kv_spec, pl.BlockSpec((heads, 1), lambda i: (0, 0))],
        out_specs=q_spec, out_shape=jax.ShapeDtypeStruct((b, heads, hd), F32),
        compiler_params=_cparams("parallel"), name="swa_sample",
    )(q, kc, vc, sinks)


def _softplus(z):
    return jnp.maximum(z, 0.0) + jnp.log1p(jnp.exp(-jnp.abs(z)))


def _rwkv_prep_kernel(*refs, d_lora, i_lora, lora_cols, tiles_per_seq, n_real_tiles, n_tail):
    groups = [refs[4 * g:4 * g + 4] for g in range(4)]
    mur_ref, muk_ref, muv_ref, mulo_ref, w0_ref, a0_ref, w2_ref, a2_ref, g2_ref = refs[16:25]
    real_out = refs[25:31]
    tail_out = refs[31:37]
    i = pl.program_id(1)
    is_tail = i == n_real_tiles
    is_start = i % tiles_per_seq == 0

    def shifted(group, mu_ref):
        u_ref, p8_ref, b_ref, t_ref = group
        u = u_ref[...]
        row = lax.broadcasted_iota(I32, u.shape, 0)
        first = jnp.where(is_start, b_ref[0:1, :], p8_ref[SUBLANES - 1:SUBLANES, :])
        prev = jnp.where(row == 0, first, pltpu.roll(u, 1, axis=0))
        prev = jnp.where(is_tail, t_ref[...], prev)
        return u + (prev - u) * mu_ref[...]

    def emit(k, val):
        @pl.when(jnp.logical_not(is_tail))
        def _():
            real_out[k][...] = val

        @pl.when(is_tail)
        def _():
            row = lax.broadcasted_iota(I32, val.shape, 0)
            tail_out[k][...] = jnp.where(row < n_tail, val, 0.0)

    emit(0, shifted(groups[0], mur_ref))
    emit(1, shifted(groups[1], muk_ref))
    emit(2, shifted(groups[2], muv_ref))
    lo = shifted(groups[3], mulo_ref)
    lane = lax.broadcasted_iota(I32, lo.shape, 1)
    lo = jnp.where(lane < lora_cols, lo, 0.0)
    d_hi = _round_up(d_lora, LANES)
    a_hi = _round_up(d_lora + i_lora, LANES)
    g_lo = (d_lora + i_lora) // LANES * LANES
    wl = w0_ref[...] + jnp.dot(jnp.tanh(lo[:, :d_hi]), w2_ref[:d_hi, :], precision=HIGHEST,
                               preferred_element_type=F32)
    w_log = -_softplus(-wl) - 0.5
    emit(3, jnp.exp(-jnp.exp(w_log)))
    al = a0_ref[...] + jnp.dot(lo[:, :a_hi], a2_ref[:a_hi, :], precision=HIGHEST,
                               preferred_element_type=F32)
    emit(4, jax.nn.sigmoid(al))
    emit(5, jnp.dot(jax.nn.sigmoid(lo[:, g_lo:]), g2_ref[g_lo:, :], precision=HIGHEST,
                    preferred_element_type=F32))


def _rwkv_prep(proj, bnd, tail_prev, mu, w0, a0, w2p, a2p, g2p, *, u_off, width, d_lora, i_lora,
               lora_cols, cw, tr, seq, n_real):
    rows = proj.shape[0]
    assert seq % tr == 0 and n_real % seq == 0 and 0 < rows - n_real <= tr
    nc = width // cw
    ob = u_off // cw
    wb = width // cw
    per8 = tr // SUBLANES
    nrt = n_real // tr

    def group(off, fixed):
        def col(c):
            return off if fixed else off + c
        return [pl.BlockSpec((tr, cw), lambda c, i: (i, ob + col(c))),
                pl.BlockSpec((SUBLANES, cw), lambda c, i: (jnp.maximum(i * per8 - 1, 0), ob + col(c))),
                pl.BlockSpec((SUBLANES, cw), lambda c, i: (0, col(c))),
                pl.BlockSpec((tr, cw), lambda c, i: (0, col(c)))]

    def vec(off, fixed=False):
        return pl.BlockSpec((1, cw), lambda c, i: (0, off if fixed else off + c))

    wspec = pl.BlockSpec((cw, cw), lambda c, i: (0, c))
    real_spec = pl.BlockSpec((tr, cw), lambda c, i: (jnp.minimum(i, nrt - 1), c))
    tail_spec = pl.BlockSpec((tr, cw), lambda c, i: (0, c))
    in_specs = (group(0, False) + group(wb, False) + group(2 * wb, False) + group(3 * wb, True)
                + [vec(0), vec(wb), vec(2 * wb), vec(3 * wb, True), vec(0), vec(0),
                   wspec, wspec, wspec])
    real = jax.ShapeDtypeStruct((n_real, width), F32)
    tail = jax.ShapeDtypeStruct((tr, width), F32)
    mu2 = mu.reshape(1, -1)
    kern = functools.partial(_rwkv_prep_kernel, d_lora=d_lora, i_lora=i_lora, lora_cols=lora_cols,
                             tiles_per_seq=seq // tr, n_real_tiles=nrt, n_tail=rows - n_real)
    return pl.pallas_call(
        kern, grid=(nc, nrt + 1), in_specs=in_specs, out_specs=[real_spec] * 6 + [tail_spec] * 6,
        out_shape=[real] * 6 + [tail] * 6, compiler_params=_cparams("arbitrary", "arbitrary"),
        name="rwkv_prep",
    )(*([proj, proj, bnd, tail_prev] * 4), mu2, mu2, mu2, mu2,
      w0.reshape(1, -1), a0.reshape(1, -1), w2p, a2p, g2p)


def _rwkv_scan_kernel(r_ref, w_ref, k_ref, v_ref, a_ref, kk_ref, ka_ref, rk_ref, gw_ref, gb_ref,
                      s0_ref, y_ref, sfin_ref, state, a_s, b_s, k_s, wr_s, *, tc, nt, hd):
    tci = pl.program_id(1)

    @pl.when(tci == 0)
    def _():
        state[...] = s0_ref[...]

    sub = lax.broadcasted_iota(I32, (SUBLANES, LANES), 0)

    def step(t, carry):
        r = r_ref[t]
        w = w_ref[t]
        k = k_ref[t]
        v = v_ref[t]
        a = a_ref[t]
        kk = k * kk_ref[...]
        nrm = jnp.sqrt(jnp.sum(kk * kk, axis=0, keepdims=True))
        kk = kk / jnp.maximum(nrm, 1e-12)
        k2 = k * (1.0 + (a - 1.0) * ka_ref[...])
        bt = kk * a
        a_s[...] = -kk
        b_s[...] = bt
        k_s[...] = k2
        wr_s[...] = w * r
        rb = jnp.sum(bt * r, axis=0, keepdims=True)
        rk = jnp.sum(k2 * r, axis=0, keepdims=True)
        bonus = jnp.sum(r * k2 * rk_ref[...], axis=0, keepdims=True)

        def vgroup(gi, c2):
            base = pl.multiple_of(gi * SUBLANES, SUBLANES)
            vt = v_ref[t, pl.ds(base, SUBLANES), :]
            yt = jnp.zeros((SUBLANES, LANES), F32)
            for j in range(SUBLANES):
                sv = state[base + j]
                sa = jnp.sum(sv * a_s[...], axis=0, keepdims=True)
                y0 = jnp.sum(sv * wr_s[...], axis=0, keepdims=True)
                vv = vt[j:j + 1, :]
                state[base + j] = sv * w_ref[t] + sa * b_s[...] + vv * k_s[...]
                yt = jnp.where(sub == j, y0 + sa * rb + vv * rk, yt)
            y_ref[t, pl.ds(base, SUBLANES), :] = yt
            return c2

        lax.fori_loop(0, hd // SUBLANES, vgroup, 0)
        y = y_ref[t]
        mu = jnp.mean(y, axis=0, keepdims=True)
        yc = y - mu
        var = jnp.mean(yc * yc, axis=0, keepdims=True)
        y_ref[t] = yc * lax.rsqrt(var + GN_EPS) * gw_ref[...] + gb_ref[...] + bonus * v
        return carry

    lax.fori_loop(0, tc, step, 0)

    @pl.when(tci == nt - 1)
    def _():
        sfin_ref[...] = state[...]


def _rwkv_scan(r, w, k, v, a, params, s0):
    t, hd, lanes = r.shape
    tc = _div_tile(t, 64, 1)
    nt = t // tc
    step_spec = pl.BlockSpec((tc, hd, LANES), lambda g, c: (c, 0, g))
    par_spec = pl.BlockSpec((hd, LANES), lambda g, c: (0, g))
    st_spec = pl.BlockSpec((hd, hd, LANES), lambda g, c: (0, 0, g))
    tile = pltpu.VMEM((hd, LANES), F32)
    return pl.pallas_call(
        functools.partial(_rwkv_scan_kernel, tc=tc, nt=nt, hd=hd),
        grid=(lanes // LANES, nt),
        in_specs=[step_spec] * 5 + [par_spec] * 5 + [st_spec],
        out_specs=[step_spec, st_spec],
        out_shape=[jax.ShapeDtypeStruct((t, hd, lanes), F32),
                   jax.ShapeDtypeStruct((hd, hd, lanes), F32)],
        scratch_shapes=[pltpu.VMEM((hd, hd, LANES), F32), tile, tile, tile, tile],
        compiler_params=_cparams("parallel", "arbitrary"), name="rwkv_scan",
    )(r, w, k, v, a, *params, s0)


def _gate_mul_kernel(a_ref, b_ref, at_ref, bt_ref, o_ref, *, n_real_tiles):
    is_tail = pl.program_id(0) == n_real_tiles
    a = jnp.where(is_tail, at_ref[...], a_ref[...])
    b = jnp.where(is_tail, bt_ref[...], b_ref[...])
    o_ref[...] = (a * b).astype(o_ref.dtype)


def _gate_mul(a_real, b_real, a_tail, b_tail, rows, tr):
    n_real, d = a_real.shape
    nrt = n_real // tr
    real = pl.BlockSpec((tr, d), lambda i: (jnp.minimum(i, nrt - 1), 0))
    tail = pl.BlockSpec((tr, d), lambda i: (0, 0))
    return pl.pallas_call(
        functools.partial(_gate_mul_kernel, n_real_tiles=nrt), grid=(nrt + 1,),
        in_specs=[real, real, tail, tail], out_specs=pl.BlockSpec((tr, d), lambda i: (i, 0)),
        out_shape=jax.ShapeDtypeStruct((rows, d), BF16),
        compiler_params=_cparams("arbitrary"), name="gate_mul")(a_real, b_real, a_tail, b_tail)


def _router_kernel(x_ref, w_ref, idx_ref, gate_ref, *, n_exp):
    logits = jnp.dot(x_ref[...], w_ref[...], precision=HIGHEST, preferred_element_type=F32)
    lane = lax.broadcasted_iota(I32, logits.shape, 1)
    lg = jnp.where(lane < n_exp, logits, -jnp.inf)
    m1 = jnp.max(lg, axis=-1, keepdims=True)
    i1 = jnp.min(jnp.where(lg == m1, lane, LANES), axis=-1, keepdims=True)
    lg2 = jnp.where(lane == i1, -jnp.inf, lg)
    m2 = jnp.max(lg2, axis=-1, keepdims=True)
    i2 = jnp.min(jnp.where(lg2 == m2, lane, LANES), axis=-1, keepdims=True)
    e = jnp.exp(m2 - m1)
    g1 = 1.0 / (1.0 + e)
    g2 = e / (1.0 + e)
    idx_ref[...] = jnp.where(lane == 0, i1, jnp.where(lane == 1, i2, 0))
    gate_ref[...] = jnp.where(lane == 0, g1, jnp.where(lane == 1, g2, 0.0))


def _router(x, w_router):
    r, d = x.shape
    n_exp = w_router.shape[1]
    wp = jnp.zeros((d, LANES), F32).at[:, :n_exp].set(w_router)
    tr = _div_tile(r, 512, SUBLANES)
    row = pl.BlockSpec((tr, d), lambda i: (i, 0))
    out = pl.BlockSpec((tr, LANES), lambda i: (i, 0))
    return pl.pallas_call(
        functools.partial(_router_kernel, n_exp=n_exp), grid=(r // tr,),
        in_specs=[row, pl.BlockSpec((d, LANES), lambda i: (0, 0))], out_specs=(out, out),
        out_shape=(jax.ShapeDtypeStruct((r, LANES), I32), jax.ShapeDtypeStruct((r, LANES), F32)),
        compiler_params=_cparams("parallel"), name="router",
    )(x, wp)


def _row_copy(src_hbm, row, dst, r, sem):
    return pltpu.make_async_copy(src_hbm.at[pl.ds(row, 1), :], dst.at[pl.ds(r, 1), :], sem)


def _dispatch_kernel(src_ref, nused_ref, x_hbm, o_ref, buf, sem, *, tg):
    base = pl.program_id(0) * tg

    @pl.when(base < nused_ref[0])
    def _():
        def issue(r, c):
            _row_copy(x_hbm, src_ref[base + r], buf, r, sem).start()
            return c

        def wait(r, c):
            _row_copy(x_hbm, 0, buf, r, sem).wait()
            return c

        lax.fori_loop(0, tg, issue, 0)
        lax.fori_loop(0, tg, wait, 0)
        o_ref[...] = buf[...].astype(BF16)

    @pl.when(base >= nused_ref[0])
    def _():
        o_ref[...] = jnp.zeros_like(o_ref)


def _dispatch(x, src, nused, *, tg):
    p = src.shape[0]
    d = x.shape[1]
    return pl.pallas_call(
        functools.partial(_dispatch_kernel, tg=tg),
        grid_spec=pltpu.PrefetchScalarGridSpec(
            num_scalar_prefetch=2, grid=(p // tg,),
            in_specs=[pl.BlockSpec(memory_space=pl.ANY)],
            out_specs=pl.BlockSpec((tg, d), lambda g, s, n: (g, 0)),
            scratch_shapes=[pltpu.VMEM((tg, d), F32), pltpu.SemaphoreType.DMA(())]),
        out_shape=jax.ShapeDtypeStruct((p, d), BF16),
        compiler_params=_cparams("arbitrary"), name="moe_dispatch",
    )(src, nused, x)


def _combine_kernel(d1_ref, d2_ref, o_hbm, gate_ref, x_ref, g_ref, b_ref, xo_ref, xb_ref,
                    buf1, buf2, sem1, sem2, *, tg, alpha):
    base = pl.program_id(0) * tg

    def issue(r, c):
        _row_copy(o_hbm, d1_ref[base + r], buf1, r, sem1).start()
        _row_copy(o_hbm, d2_ref[base + r], buf2, r, sem2).start()
        return c

    def wait(r, c):
        _row_copy(o_hbm, 0, buf1, r, sem1).wait()
        _row_copy(o_hbm, 0, buf2, r, sem2).wait()
        return c

    lax.fori_loop(0, tg, issue, 0)
    lax.fori_loop(0, tg, wait, 0)
    gates = gate_ref[...]
    y = gates[:, 0:1] * buf1[...] + gates[:, 1:2] * buf2[...]
    out = _ln_rows(alpha * x_ref[...] + y, g_ref[...], b_ref[...])
    xo_ref[...] = out
    xb_ref[...] = out.astype(BF16)


def _combine_ln(o_sorted, d1, d2, gates, x, g, b, *, alpha, tg):
    r, d = x.shape
    row = pl.BlockSpec((tg, d), lambda i, a, c: (i, 0))
    vec = pl.BlockSpec((1, d), lambda i, a, c: (0, 0))
    return pl.pallas_call(
        functools.partial(_combine_kernel, tg=tg, alpha=alpha),
        grid_spec=pltpu.PrefetchScalarGridSpec(
            num_scalar_prefetch=2, grid=(r // tg,),
            in_specs=[pl.BlockSpec(memory_space=pl.ANY),
                      pl.BlockSpec((tg, LANES), lambda i, a, c: (i, 0)), row, vec, vec],
            out_specs=(row, row),
            scratch_shapes=[pltpu.VMEM((tg, d), F32), pltpu.VMEM((tg, d), F32),
                            pltpu.SemaphoreType.DMA(()), pltpu.SemaphoreType.DMA(())]),
        out_shape=(jax.ShapeDtypeStruct((r, d), F32), jax.ShapeDtypeStruct((r, d), BF16)),
        compiler_params=_cparams("arbitrary"), name="moe_combine_ln",
    )(d1, d2, o_sorted, gates, x, g.reshape(1, d), b.reshape(1, d))


def _route_plan(idx, n_exp, tm):
    r = idx.shape[0]
    flat = idx.reshape(-1)
    onehot = (flat[:, None] == jnp.arange(n_exp, dtype=I32)[None, :]).astype(I32)
    rank = jnp.sum((jnp.cumsum(onehot, axis=0) - onehot) * onehot, axis=1)
    sizes = jnp.sum(onehot, axis=0)
    etiles = (sizes + tm - 1) // tm
    tend = jnp.cumsum(etiles)
    tstart = tend - etiles
    dest = tstart[flat] * tm + rank
    nt = pl.cdiv(2 * r, tm) + n_exp
    tid = jnp.arange(nt, dtype=I32)
    n_used = tend[-1]
    te = jnp.minimum(jnp.sum((tid[:, None] >= tend[None, :]).astype(I32), axis=1), n_exp - 1)
    used = tid < n_used
    last = jnp.maximum(n_used - 1, 0)
    tidx = jnp.minimum(tid, last)
    texp = te[tidx]
    valid = jnp.where(used, jnp.clip(sizes[texp] - (tid - tstart[texp]) * tm, 0, tm), 0)
    prev = jnp.concatenate([jnp.full((1,), -1, I32), texp[:-1]])
    wnew = jnp.logical_and(used, texp != prev).astype(I32)
    src = jnp.zeros((nt * tm,), I32).at[dest].set(jnp.arange(2 * r, dtype=I32) // 2)
    sched = (valid.astype(I32), tidx.astype(I32), texp.astype(I32), wnew)
    dest2 = dest.reshape(r, 2).astype(I32)
    return sched, src, (n_used * tm).reshape(1).astype(I32), dest2[:, 0], dest2[:, 1]


def kernel(x_prompt, x_sample, cache_swa_k, cache_swa_v, state_rwkv, state_shift, meta_tokens, ln_emb_g, ln_emb_b, w_in, att_sinks, rwkv_mu, rwkv_w0, rwkv_w2, rwkv_a0, rwkv_a2, rwkv_g2, rwkv_k_k, rwkv_k_a, rwkv_r_k, rwkv_gn_w, rwkv_gn_b, w_proj_att, w_proj_rwkv, w_out, ln_mix_g, ln_mix_b, ln_ffn_g, ln_ffn_b, ffn_w_gate, ffn_w_up, ffn_w_down, moe_router, moe_w_gate, moe_w_up, moe_w_down):
    nb, seq, dm = x_prompt.shape
    db, dseq, _ = x_sample.shape
    assert dseq == 1
    depth = w_in.shape[0]
    n_meta = meta_tokens.shape[0]
    _, _, window, kvh, hd = cache_swa_k.shape
    heads = att_sinks.shape[1]
    rh = state_rwkv.shape[2]
    att_w = heads * hd
    kv_w = kvh * hd
    rw = rh * hd
    shift_cols = state_shift.shape[2]
    lora_cols = shift_cols - 3 * rw
    d_lora, i_lora, g_lora = rwkv_w2.shape[1], rwkv_a2.shape[1], rwkv_g2.shape[1]
    assert d_lora + i_lora + g_lora == lora_cols
    n_exp = moe_router.shape[2]
    blk = window
    assert seq % blk == 0 and n_meta <= blk and (nb * rh) % LANES == 0 and (db * rh) % LANES == 0
    nblk = seq // blk
    lead = blk - n_meta
    alpha = (2 * depth) ** 0.25

    nr = nb * seq
    m0, s0r = nr, nr + n_meta
    rows = nr + _round_up(n_meta + db, 64)
    u_off = att_w + 2 * kv_w
    g_off = u_off + shift_cols
    cw = 512
    assert u_off % cw == 0 and rw % cw == 0 and lora_cols <= cw

    tm = 1024
    tr = 256
    tk_down = 1024
    sched = _dense_sched(rows, tm)

    def tail_rows(meta_part, sample_part):
        cols = meta_part.shape[1]
        return jnp.concatenate([meta_part, sample_part,
                                jnp.zeros((tr - n_meta - db, cols), F32)], axis=0)

    x, xb = _embed_ln(x_prompt.reshape(nr, dm), tail_rows(meta_tokens, x_sample.reshape(db, dm)),
                      ln_emb_g, ln_emb_b, rows, tr)

    def to_lanes_prompt(z):
        return z.reshape(nb, seq, rh, hd).transpose(1, 3, 0, 2).reshape(seq, hd, nb * rh)

    def to_lanes_meta(z):
        m = z[:n_meta].reshape(n_meta, rh, hd).transpose(0, 2, 1)
        return jnp.tile(m, (1, 1, nb))

    def to_lanes_sample(z):
        return z[n_meta:n_meta + db].reshape(db, rh, hd).transpose(2, 0, 1).reshape(1, hd, db * rh)

    def head_param(p, nbatch):
        return jnp.broadcast_to(p.reshape(rh, hd).T[:, None, :], (hd, nbatch, rh)).reshape(hd, nbatch * rh)

    w_main = w_in[:, :, :_round_up(g_off, LANES)]
    w_gates = w_in[:, :, g_off:]

    outs = {k: [] for k in ("pk", "pv", "ps", "psh", "sk", "sv", "ss", "ssh")}
    for l in range(depth):
        proj = _matmul_ws([xb], [(w_main, (l,))], [(0, 0)], [], _ep_first, F32, g_off,
                          tm=tm, tn=512, sched=sched, name="proj_in")
        gates = _matmul_ws([xb], [(w_gates, (l,))], [(0, 0)], [], _ep_first, F32,
                           2 * dm, tm=tm, tn=512, sched=sched, name="proj_gates")

        k_all = proj[:, att_w:att_w + kv_w]
        v_all = proj[:, att_w + kv_w:att_w + 2 * kv_w]
        sinks = att_sinks[l]

        def padded(z):
            lead_blk = jnp.concatenate([jnp.zeros((lead, kv_w), F32), z[m0:m0 + n_meta]], axis=0)
            real = z[:nr].reshape(nb, seq, kv_w)
            full = jnp.concatenate([jnp.broadcast_to(lead_blk[None], (nb, blk, kv_w)), real], axis=1)
            meta_only = jnp.concatenate([jnp.zeros((blk, kv_w), F32), lead_blk], axis=0)[None]
            return full, meta_only

        k_pad, k_meta = padded(k_all)
        v_pad, v_meta = padded(v_all)
        swa = functools.partial(_swa_prompt, lead=lead, heads=heads, kvh=kvh, hd=hd, blk=blk)
        o_real = swa(proj, k_pad, v_pad, sinks, nbatch=nb, nblk=nblk, n_off=0)
        q_meta = jnp.concatenate([jnp.zeros((lead, att_w), F32), proj[m0:m0 + n_meta, :att_w]], axis=0)
        o_meta = swa(q_meta, k_meta, v_meta, sinks, nbatch=1, nblk=1, n_off=-1)[lead:]

        k_new = k_all[s0r:s0r + db].reshape(db, 1, kvh, hd)
        v_new = v_all[s0r:s0r + db].reshape(db, 1, kvh, hd)
        kc = jnp.concatenate([cache_swa_k[l][:, 1:], k_new], axis=1)
        vc = jnp.concatenate([cache_swa_v[l][:, 1:], v_new], axis=1)
        q_s = proj[s0r:s0r + db, :att_w].reshape(db, heads, hd)
        o_s = _swa_sample(q_s, kc.transpose(0, 2, 1, 3), vc.transpose(0, 2, 1, 3),
                          sinks.reshape(heads, 1))
        y_att = jnp.concatenate([o_real, o_meta, o_s.reshape(db, att_w).astype(BF16),
                                 jnp.zeros((rows - s0r - db, att_w), BF16)], axis=0)
        outs["pk"].append(k_all[:nr].reshape(nb, seq, kvh, hd)[:, seq - window:])
        outs["pv"].append(v_all[:nr].reshape(nb, seq, kvh, hd)[:, seq - window:])
        outs["sk"].append(kc)
        outs["sv"].append(vc)

        u_meta = proj[m0:m0 + n_meta, u_off:u_off + shift_cols]
        u_sample = proj[s0r:s0r + db, u_off:u_off + shift_cols]
        u_last = proj[seq - 1:nr:seq, u_off:u_off + shift_cols]
        bnd = jnp.concatenate([u_meta[n_meta - 1:], jnp.zeros((SUBLANES - 1, shift_cols), F32)], axis=0)
        tail_prev = tail_rows(
            jnp.concatenate([jnp.zeros((1, shift_cols), F32), u_meta[:-1]], axis=0), state_shift[l])
        w2p = jnp.zeros((cw, rw), F32).at[:d_lora].set(rwkv_w2[l])
        a2p = jnp.zeros((cw, rw), F32).at[d_lora:d_lora + i_lora].set(rwkv_a2[l])
        g2p = jnp.zeros((cw, rw), F32).at[d_lora + i_lora:lora_cols].set(rwkv_g2[l])
        prep = _rwkv_prep(
            proj, bnd, tail_prev, rwkv_mu[l], rwkv_w0[l], rwkv_a0[l], w2p, a2p, g2p,
            u_off=u_off, width=rw, d_lora=d_lora, i_lora=i_lora, lora_cols=lora_cols, cw=cw,
            tr=tr, seq=seq, n_real=nr)
        real_in = [prep[j] for j in (0, 3, 1, 2, 4)]
        tail_in = [prep[6 + j] for j in (0, 3, 1, 2, 4)]
        g_real, g_tail = prep[5], prep[11]

        hp = (rwkv_k_k[l], rwkv_k_a[l], rwkv_r_k[l], rwkv_gn_w[l], rwkv_gn_b[l])
        hp_prompt = [head_param(p, nb) for p in hp]
        y_m, s_m = _rwkv_scan(*[to_lanes_meta(z) for z in tail_in], hp_prompt,
                              jnp.zeros((hd, hd, nb * rh), F32))
        y_p, s_p = _rwkv_scan(*[to_lanes_prompt(z) for z in real_in], hp_prompt, s_m)
        y_s, s_s = _rwkv_scan(*[to_lanes_sample(z) for z in tail_in],
                              [head_param(p, db) for p in hp],
                              state_rwkv[l].transpose(2, 3, 0, 1).reshape(hd, hd, db * rh))
        y_real = y_p.reshape(seq, hd, nb, rh).transpose(2, 0, 3, 1).reshape(nr, rw)
        y_tail = tail_rows(y_m[:, :, :rh].transpose(0, 2, 1).reshape(n_meta, rw),
                           y_s.reshape(hd, db, rh).transpose(1, 2, 0).reshape(db, rw))
        yg = _gate_mul(y_real, g_real, y_tail, g_tail, rows, tr)
        outs["ps"].append(s_p.reshape(hd, hd, nb, rh).transpose(2, 3, 0, 1))
        outs["ss"].append(s_s.reshape(hd, hd, db, rh).transpose(2, 3, 0, 1))
        outs["psh"].append(u_last)
        outs["ssh"].append(u_sample)

        tn_merge = min(512, dm)
        merged = _matmul_ws([y_att, yg], [(w_proj_att, (l,)), (w_proj_rwkv, (l,))],
                            [(0, 0), (1, 1)], [(gates, 0), (gates, dm // tn_merge)], _ep_merge,
                            BF16, dm, tm=tm, tn=tn_merge, sched=sched, name="merge")
        mix = _matmul_ws([merged], [(w_out, (l,))], [(0, 0)], [], _ep_first, F32, dm,
                         tm=tm, tn=512, sched=sched, name="proj_out")
        x, xb = _res_layer_norm(x, mix, ln_mix_g[l], ln_mix_b[l], alpha)

        i = l // 2
        if l % 2 == 0:
            h = _matmul_ws([xb], [(ffn_w_gate, (i,)), (ffn_w_up, (i,))], [(0, 0), (0, 1)], [],
                           _ep_swiglu, BF16, _round_up(ffn_w_gate.shape[2], tk_down), tm=tm,
                           tn=256, sched=sched, name="ffn_gate_up")
            y = _matmul_kacc(h, ffn_w_down, (i,), tm=tm, tn=2048, tk=tk_down, sched=sched,
                             name="ffn_down")
            x, xb = _res_layer_norm(x, y, ln_ffn_g[l], ln_ffn_b[l], alpha)
        else:
            idx, gates = _router(x, moe_router[i])
            msched, src, nused, d1, d2 = _route_plan(idx[:, :2], n_exp, tm)
            xs = _dispatch(x, src, nused, tg=256)
            h = _matmul_ws([xs], [(moe_w_gate, (i, "e")), (moe_w_up, (i, "e"))],
                           [(0, 0), (0, 1)], [], _ep_swiglu, BF16,
                           _round_up(moe_w_gate.shape[3], tk_down), tm=tm, tn=256, sched=msched,
                           name="moe_gate_up")
            o = _matmul_kacc(h, moe_w_down, (i, "e"), tm=tm, tn=2048, tk=tk_down, sched=msched,
                             name="moe_down")
            x, xb = _combine_ln(o, d1, d2, gates, x, ln_ffn_g[l], ln_ffn_b[l], alpha=alpha,
                                tg=_div_tile(rows, 128, SUBLANES))

    y_prompt = x[:nr].reshape(nb, seq, dm)
    y_sample = x[s0r:s0r + db].reshape(db, 1, dm)
    st = {k: jnp.stack(v) for k, v in outs.items()}
    return (y_prompt, y_sample, st["pk"], st["pv"], st["ps"], st["psh"],
            st["sk"], st["sv"], st["ss"], st["ssh"])
```
